```python
import math
import jax, jax.numpy as jnp
from jax import lax
import numpy as np

D_MODEL = 1024
BATCH = 32
SEQ = 256
DEPTH = 1
DEC_BATCH = 2
DEC_SEQ = 4096
PAST_LEN = 256

GRID_W = 64
N_DIR = 2
N_DN_HEADS = 8
DN_HEAD_DIM = 128
D_DN = N_DN_HEADS * DN_HEAD_DIM
D_RG = 1024
N_RG_BLOCKS = 8
RG_BLOCK = D_RG // N_RG_BLOCKS
D_MIX = D_DN + D_RG
CONV_W = 4
CONV_PAD_L = 2
CONV_PAD_R = CONV_W - 1 - CONV_PAD_L
CHUNK = 64
RG_C = 8.0
EPS = 1e-6
OFF_Z = 3 * D_DN
OFF_B = 4 * D_DN
OFF_A = OFF_B + N_DIR * N_DN_HEADS
OFF_RX = OFF_A + N_DIR * N_DN_HEADS
OFF_RG = OFF_RX + D_RG
D_IN = OFF_RG + D_RG

kernel_name = 'hybrid_deltanet_rglru_diffusion_step'


def rmsnorm(x, w):
    xf = x.astype(jnp.float32)
    y = xf * lax.rsqrt(jnp.mean(xf * xf, axis=-1, keepdims=True) + EPS)
    return (y * w.astype(jnp.float32)).astype(x.dtype)


def l2norm(x):
    return x * lax.rsqrt(jnp.sum(x * x, axis=-1, keepdims=True) + EPS)


def short_conv(x, w):
    t = x.shape[1]
    xp = jnp.pad(x, ((0, 0), (CONV_PAD_L, CONV_PAD_R), (0, 0)))
    return sum(xp[:, j:j + t, :] * w[j] for j in range(CONV_W))


def grid_pos_embed(n_tok, dtype):
    rows = n_tok // GRID_W
    r = jnp.repeat(jnp.arange(rows, dtype=jnp.float32), GRID_W)
    col = jnp.tile(jnp.arange(GRID_W, dtype=jnp.float32), rows)
    quarter = D_MODEL // 4
    freqs = jnp.exp(-math.log(10000.0) * jnp.arange(quarter, dtype=jnp.float32) / quarter)
    ar = r[:, None] * freqs[None]
    ac = col[:, None] * freqs[None]
    pe = jnp.concatenate([jnp.sin(ar), jnp.cos(ar), jnp.sin(ac), jnp.cos(ac)], axis=-1)
    return pe.astype(dtype)


def gated_delta_chunked(q, k, v, g, beta, s0):
    b, t, h, dk = q.shape
    dv = v.shape[-1]
    n = t // CHUNK

    def chunks(a):
        return jnp.moveaxis(a.reshape((b, n, CHUNK, h) + a.shape[3:]), 3, 1)

    q = chunks(q) * (dk ** -0.5)
    k = chunks(k)
    v = chunks(v)
    beta = chunks(beta)
    g = jnp.cumsum(chunks(g), axis=-1)
    tril = jnp.tril(jnp.ones((CHUNK, CHUNK), dtype=bool))
    strict = jnp.tril(jnp.ones((CHUNK, CHUNK), dtype=bool), -1)
    decay = jnp.exp(jnp.where(tril, g[..., :, None] - g[..., None, :], -jnp.inf))
    kk = jnp.einsum('bhncd,bhnsd->bhncs', k * beta[..., None], k) * decay
    tmat = jnp.where(strict, kk, 0.0) + jnp.eye(CHUNK, dtype=jnp.float32)
    rhs = jnp.concatenate([v * beta[..., None], k * (beta * jnp.exp(g))[..., None]], axis=-1)
    sol = lax.linalg.triangular_solve(tmat, rhs, left_side=True, lower=True, unit_diagonal=True)
    u, w = sol[..., :dv], sol[..., dv:]
    attn = jnp.where(tril, jnp.einsum('bhncd,bhnsd->bhncs', q, k) * decay, 0.0)
    q_dec = q * jnp.exp(g)[..., None]
    k_dec = k * jnp.exp(g[..., -1:] - g)[..., None]
    g_last = jnp.exp(g[..., -1])

    def step(s, xs):
        u_i, w_i, attn_i, qd_i, kd_i, gl_i = xs
        v_new = u_i - jnp.einsum('bhcd,bhde->bhce', w_i, s)
        o = jnp.einsum('bhcd,bhde->bhce', qd_i, s) + jnp.einsum('bhcs,bhse->bhce', attn_i, v_new)
        s = s * gl_i[..., None, None] + jnp.einsum('bhcd,bhce->bhde', kd_i, v_new)
        return s, o

    xs = tuple(jnp.moveaxis(a, 2, 0) for a in (u, w, attn, q_dec, k_dec, g_last))
    s_fin, o = lax.scan(step, s0, xs)
    o = jnp.moveaxis(jnp.moveaxis(o, 0, 2), 1, 3).reshape(b, t, h, dv)
    return o, s_fin


def delta_branch(qkv, z, b_raw, a_raw, conv_w, a_log, dt_bias, norm_w, s0):
    bsz, t, _ = qkv.shape
    qkv = jax.nn.silu(short_conv(qkv, conv_w)).astype(jnp.float32)
    q, k, v = jnp.split(qkv.reshape(bsz, t, 3, N_DN_HEADS, DN_HEAD_DIM), 3, axis=2)
    q, k, v = l2norm(q[:, :, 0]), l2norm(k[:, :, 0]), v[:, :, 0]
    beta = jax.nn.sigmoid(b_raw.astype(jnp.float32))
    g = -jnp.exp(a_log) * jax.nn.softplus(a_raw.astype(jnp.float32) + dt_bias)
    s0 = s0.astype(jnp.float32)
    o_f, s_f = gated_delta_chunked(q, k, v, g[:, :, 0], beta[:, :, 0], s0[:, 0])
    flip = lambda a: jnp.flip(a, axis=1)
    o_b, s_b = gated_delta_chunked(flip(q), flip(k), flip(v), flip(g[:, :, 1]), flip(beta[:, :, 1]), s0[:, 1])
    o = rmsnorm(o_f + flip(o_b), norm_w) * jax.nn.silu(z.astype(jnp.float32).reshape(bsz, t, N_DN_HEADS, DN_HEAD_DIM))
    return o.reshape(bsz, t, D_DN), jnp.stack([s_f, s_b], axis=1)


def linear_scan(a, bx, h0):
    bx = bx.at[:, 0].add(a[:, 0] * h0)

    def comb(l, r):
        return l[0] * r[0], r[0] * l[1] + r[1]

    _, h = lax.associative_scan(comb, (a, bx), axis=1)
    return h


def rglru_branch(x, gate, conv_w, conv_b, w_r, b_r, w_i, b_i, lam, h0):
    bsz, t, _ = x.shape
    xf = (short_conv(x, conv_w) + conv_b).astype(jnp.float32)
    xb = xf.reshape(bsz, t, N_RG_BLOCKS, RG_BLOCK)
    r = jax.nn.sigmoid(jnp.einsum('btnk,dnkj->btdnj', xb, w_r).reshape(bsz, t, N_DIR, D_RG) + b_r)
    i = jax.nn.sigmoid(jnp.einsum('btnk,dnkj->btdnj', xb, w_i).reshape(bsz, t, N_DIR, D_RG) + b_i)
    log_a = -RG_C * r * jax.nn.softplus(-lam)
    a = jnp.exp(log_a)
    bx = jnp.sqrt(-jnp.expm1(2.0 * log_a)) * (i * xf[:, :, None, :])
    h0 = h0.astype(jnp.float32)
    h_f = linear_scan(a[:, :, 0], bx[:, :, 0], h0[:, 0])
    h_b = jnp.flip(linear_scan(jnp.flip(a[:, :, 1], 1), jnp.flip(bx[:, :, 1], 1), h0[:, 1]), 1)
    y = (h_f + h_b) * jax.nn.silu(gate.astype(jnp.float32))
    return y, jnp.stack([h_f[:, -1], h_b[:, 0]], axis=1)


def mixer_layer(x, mod, s_dn0, s_rg0, pre_w, post_w, w_in, w_out, dn_conv_w, dn_a_log, dn_dt_bias,
                dn_norm_w, rg_conv_w, rg_conv_b, rg_w_r, rg_b_r, rg_w_i, rg_b_i, rg_lam):
    bsz, t, _ = x.shape
    shift, scale, gate = jnp.split(mod[:, None, :].astype(x.dtype), 3, axis=-1)
    h = rmsnorm(x, pre_w) * (1 + scale) + shift
    proj = h @ w_in
    o_dn, s_dn = delta_branch(
        proj[..., :OFF_Z], proj[..., OFF_Z:OFF_B],
        proj[..., OFF_B:OFF_A].reshape(bsz, t, N_DIR, N_DN_HEADS),
        proj[..., OFF_A:OFF_RX].reshape(bsz, t, N_DIR, N_DN_HEADS),
        dn_conv_w, dn_a_log, dn_dt_bias, dn_norm_w, s_dn0)
    o_rg, s_rg = rglru_branch(proj[..., OFF_RX:OFF_RG], proj[..., OFF_RG:], rg_conv_w, rg_conv_b,
                              rg_w_r, rg_b_r, rg_w_i, rg_b_i, rg_lam, s_rg0)
    y = jnp.concatenate([o_dn, o_rg], axis=-1).astype(x.dtype) @ w_out
    return x + gate * rmsnorm(y, post_w), s_dn, s_rg


def setup_inputs(seed: int = 0) -> dict:
    key = jax.random.key(seed)
    ks = jax.random.split(key, 23)
    f32 = jnp.float32
    nrm = lambda k, shape, s: s * jax.random.normal(k, shape, f32)
    L = DEPTH
    dt = jnp.exp(jax.random.uniform(ks[14], (L, N_DIR, N_DN_HEADS), f32, math.log(1e-3), math.log(1e-1)))
    u = jax.random.uniform(ks[22], (L, N_DIR, D_RG), f32, 0.9, 0.999)
    log_sig = jnp.log(u) / RG_C
    return {
        'x_prompt': nrm(ks[0], (BATCH, SEQ, D_MODEL), 1.0),
        'x_sample': nrm(ks[1], (DEC_BATCH, DEC_SEQ, D_MODEL), 1.0),
        'state_delta': nrm(ks[2], (DEC_BATCH, L, N_DIR, N_DN_HEADS, DN_HEAD_DIM, DN_HEAD_DIM), 0.3),
        'state_rglru': nrm(ks[3], (DEC_BATCH, L, N_DIR, D_RG), 0.5),
        'c': nrm(ks[4], (DEC_BATCH, D_MODEL), 1.0),
        'c_ctx': nrm(ks[5], (D_MODEL,), 1.0),
        'ada_w': nrm(ks[6], (L, D_MODEL, 3 * D_MODEL), D_MODEL ** -0.5),
        'ada_b': nrm(ks[7], (L, 3 * D_MODEL), 0.02),
        'pre_norm_w': 1.0 + nrm(ks[8], (L, D_MODEL), 0.02),
        'post_norm_w': 1.0 + nrm(ks[9], (L, D_MODEL), 0.02),
        'w_in': nrm(ks[10], (L, D_MODEL, D_IN), D_MODEL ** -0.5),
        'w_out': nrm(ks[11], (L, D_MIX, D_MODEL), D_MIX ** -0.5),
        'dn_conv_w': nrm(ks[12], (L, CONV_W, 3 * D_DN), CONV_W ** -0.5),
        'dn_a_log': jnp.log(jax.random.uniform(ks[13], (L, N_DIR, N_DN_HEADS), f32, 1.0, 16.0)),
        'dn_dt_bias': dt + jnp.log(-jnp.expm1(-dt)),
        'dn_norm_w': 1.0 + nrm(ks[15], (L, DN_HEAD_DIM), 0.02),
        'rg_conv_w': nrm(ks[16], (L, CONV_W, D_RG), CONV_W ** -0.5),
        'rg_conv_b': nrm(ks[17], (L, D_RG), 0.01),
        'rg_w_r': nrm(ks[18], (L, N_DIR, N_RG_BLOCKS, RG_BLOCK, RG_BLOCK), RG_BLOCK ** -0.5),
        'rg_b_r': nrm(ks[19], (L, N_DIR, D_RG), 0.01),
        'rg_w_i': nrm(ks[20], (L, N_DIR, N_RG_BLOCKS, RG_BLOCK, RG_BLOCK), RG_BLOCK ** -0.5),
        'rg_b_i': nrm(ks[21], (L, N_DIR, D_RG), 0.01),
        'rg_lam': log_sig - jnp.log(-jnp.expm1(log_sig)),
    }


def reference(x_prompt, x_sample, state_delta, state_rglru, c, c_ctx, ada_w, ada_b, pre_norm_w,
              post_norm_w, w_in, w_out, dn_conv_w, dn_a_log, dn_dt_bias, dn_norm_w, rg_conv_w,
              rg_conv_b, rg_w_r, rg_b_r, rg_w_i, rg_b_i, rg_lam):
    xp = x_prompt
    xs = x_sample + grid_pos_embed(x_sample.shape[1], x_sample.dtype)[None]
    bp = x_prompt.shape[0]
    new_dn, new_rg = [], []
    for l in range(DEPTH):
        params = (pre_norm_w[l], post_norm_w[l], w_in[l], w_out[l], dn_conv_w[l], dn_a_log[l],
                  dn_dt_bias[l], dn_norm_w[l], rg_conv_w[l], rg_conv_b[l], rg_w_r[l], rg_b_r[l],
                  rg_w_i[l], rg_b_i[l], rg_lam[l])
        mod_ctx = jax.nn.silu(c_ctx[None]) @ ada_w[l] + ada_b[l]
        mod_lat = jax.nn.silu(c) @ ada_w[l] + ada_b[l]
        zero_dn = jnp.zeros((bp, N_DIR, N_DN_HEADS, DN_HEAD_DIM, DN_HEAD_DIM), jnp.float32)
        zero_rg = jnp.zeros((bp, N_DIR, D_RG), jnp.float32)
        xp, s_dn, s_rg = mixer_layer(xp, mod_ctx, zero_dn, zero_rg, *params)
        xs, _, _ = mixer_layer(xs, mod_lat, state_delta[:, l], state_rglru[:, l], *params)
        new_dn.append(s_dn.astype(x_prompt.dtype))
        new_rg.append(s_rg.astype(x_prompt.dtype))
    new_state_delta = jnp.stack(new_dn, axis=1)
    new_state_rglru = jnp.stack(new_rg, axis=1)
    return (xp, xs, new_state_delta, new_state_rglru)
```

```python
import functools
import math

import jax
import jax.numpy as jnp
from jax import lax
from jax.experimental import pallas as pl
from jax.experimental.pallas import tpu as pltpu

F32 = jnp.float32
BF16 = jnp.bfloat16

D_MODEL = 1024
N_HEADS = 8
HEAD_DIM = 128
D_DN = N_HEADS * HEAD_DIM
D_RG = 1024
N_RG_BLOCKS = 8
RG_BLOCK = D_RG // N_RG_BLOCKS
CONV_W = 4
CONV_PAD_L = 2
CHUNK = 64
RG_C = 8.0
EPS = 1e-6
GRID_W = 64
OFF_Z = 3 * D_DN
OFF_B = 4 * D_DN
OFF_A = OFF_B + 2 * N_HEADS
OFF_RX = OFF_A + 2 * N_HEADS
D_IN = OFF_RX + 2 * D_RG

TILE = 256
HALO = 16
N_CHUNKS = TILE // CHUNK
SEG = 512
GB_LANES = 8
P_Z = 3 * D_DN
P_RX = P_Z + D_DN
P_RG = P_RX + D_RG
P_SMALL = P_RG + D_RG
D_INP = P_SMALL + 128
NEG_BIG = -1e30
VMEM_LIMIT = 56 * 1024 * 1024


def _sigmoid(x):
    return 0.5 * jnp.tanh(0.5 * x) + 0.5


def _neg_expm1(y):
    t = jnp.tanh(0.5 * y)
    return -2.0 * t / (1.0 - t)


def _silu(x):
    return x * _sigmoid(x)


def _softplus(x):
    return jnp.maximum(x, 0.0) + jnp.log1p(jnp.exp(-jnp.abs(x)))


def _mm(a, b):
    return jnp.dot(a.astype(BF16), b.astype(BF16), preferred_element_type=F32)


def _const_spec(shape):
    nd = len(shape)
    return pl.BlockSpec(shape, lambda *_: (0,) * nd, pipeline_mode=pl.Buffered(1))


def _pe_table():
    quarter = D_MODEL // 4
    j = lax.broadcasted_iota(jnp.int32, (GRID_W, quarter), 0).astype(F32)
    kf = lax.broadcasted_iota(jnp.int32, (GRID_W, quarter), 1).astype(F32)
    freqs = jnp.exp(-math.log(10000.0) * kf / quarter)
    ang = j * freqs
    return jnp.concatenate([jnp.sin(ang), jnp.cos(ang)], axis=1)


def _pe_rows(tab_ref, grid_row, n_rows, col0):
    row_part = jnp.broadcast_to(tab_ref[pl.ds(grid_row, 1), :], (n_rows, D_MODEL // 2))
    col_part = tab_ref[col0:col0 + n_rows, :]
    return jnp.concatenate([row_part, col_part], axis=1)


def _pe_tile(tab_ref, i):
    r0 = i * (TILE // GRID_W)
    return jnp.concatenate([_pe_rows(tab_ref, r0 + j, GRID_W, 0) for j in range(TILE // GRID_W)], axis=0)


def _mods_kernel(c_ref, w_ref, b_ref, o_ref):
    o_ref[...] = _mm(_silu(c_ref[...]), w_ref[...]) + b_ref[...]


def _mods(c_all, ada_w, ada_b):
    return pl.pallas_call(
        _mods_kernel,
        grid=(3,),
        in_specs=[
            pl.BlockSpec((8, D_MODEL), lambda j: (0, 0)),
            pl.BlockSpec((D_MODEL, D_MODEL), lambda j: (0, j)),
            pl.BlockSpec((1, D_MODEL), lambda j: (0, j)),
        ],
        out_specs=pl.BlockSpec((8, D_MODEL), lambda j: (0, j)),
        out_shape=jax.ShapeDtypeStruct((8, 3 * D_MODEL), F32),
        compiler_params=pltpu.CompilerParams(dimension_semantics=("arbitrary",), vmem_limit_bytes=VMEM_LIMIT),
        name="mods",
    )(c_all, ada_w, ada_b)


def _in_proj_kernel(xp_ref, x_ref, xn_ref, mods_ref, prew_ref, win_ref, dncw_ref, rgcw_ref, rgcb_ref,
                    alog_ref, dtb_ref,
                    q_ref, k_ref, v_ref, z_ref, xf_ref, gate_ref, gb_ref,
                    h_scr, p_scr, tab_scr, *, is_lat, mod_row0, n_tiles):
    b = pl.program_id(0)
    i = pl.program_id(1)

    if is_lat:
        @pl.when((b == 0) & (i == 0))
        def _():
            tab_scr[...] = _pe_table()

    mod = mods_ref[pl.ds(mod_row0 + (b if is_lat else 0), 1), :]
    shift = mod[:, :D_MODEL]
    scale = mod[:, D_MODEL:2 * D_MODEL]
    prew = prew_ref[...]

    def norm_mod(xv):
        ms = jnp.mean(xv * xv, axis=-1, keepdims=True)
        return (xv * lax.rsqrt(ms + EPS) * prew) * (1.0 + scale) + shift

    x_prev = xp_ref[0]
    x_main = x_ref[0]
    x_next = xn_ref[0]
    if is_lat:
        r0 = i * (TILE // GRID_W)
        x_main = x_main + _pe_tile(tab_scr, i)
        x_prev = x_prev + _pe_rows(tab_scr, jnp.maximum(r0 - 1, 0), HALO, GRID_W - HALO)
        x_next = x_next + _pe_rows(tab_scr, jnp.minimum(r0 + TILE // GRID_W, GRID_W - 1), HALO, 0)

    h_scr[0:HALO, :] = jnp.where(i > 0, norm_mod(x_prev), 0.0).astype(BF16)
    h_scr[HALO:HALO + TILE, :] = norm_mod(x_main).astype(BF16)
    h_scr[HALO + TILE:, :] = jnp.where(i < n_tiles - 1, norm_mod(x_next), 0.0).astype(BF16)

    def conv_seg(col0, cw_ref, cw_col0):
        p_scr[...] = jnp.dot(h_scr[...], win_ref[:, col0:col0 + SEG], preferred_element_type=F32)
        acc = None
        for j in range(CONV_W):
            tap = p_scr[pl.ds(HALO - CONV_PAD_L + j, TILE), :] * cw_ref[j:j + 1, cw_col0:cw_col0 + SEG]
            acc = tap if acc is None else acc + tap
        return acc

    def plain_seg(col0, width):
        return jnp.dot(h_scr[HALO:HALO + TILE, :], win_ref[:, col0:col0 + width], preferred_element_type=F32)

    heads_per_seg = SEG // HEAD_DIM
    for s in range(3 * D_DN // SEG):
        act = _silu(conv_seg(s * SEG, dncw_ref, s * SEG))
        which = (s * SEG) // D_DN
        out_ref = (q_ref, k_ref, v_ref)[which]
        for hl in range(heads_per_seg):
            head = ((s * SEG) % D_DN) // HEAD_DIM + hl
            xh = act[:, hl * HEAD_DIM:(hl + 1) * HEAD_DIM]
            if which < 2:
                xh = xh * lax.rsqrt(jnp.sum(xh * xh, axis=-1, keepdims=True) + EPS)
            out_ref[head] = xh

    for s in range(D_DN // SEG):
        z_ref[:, s * SEG:(s + 1) * SEG] = plain_seg(P_Z + s * SEG, SEG)
    for s in range(D_RG // SEG):
        xf_ref[:, s * SEG:(s + 1) * SEG] = (conv_seg(P_RX + s * SEG, rgcw_ref, s * SEG)
                                            + rgcb_ref[:, s * SEG:(s + 1) * SEG])
    for s in range(D_RG // SEG):
        gate_ref[:, s * SEG:(s + 1) * SEG] = plain_seg(P_RG + s * SEG, SEG)

    raw = plain_seg(P_SMALL, 128)
    beta = _sigmoid(raw)
    g = -jnp.exp(alog_ref[...]) * _softplus(raw + dtb_ref[...])
    r = lax.broadcasted_iota(jnp.int32, (TILE, TILE), 0)
    c = lax.broadcasted_iota(jnp.int32, (TILE, TILE), 1)
    same = (r // CHUNK) == (c // CHUNK)
    lower = jnp.where(same & (r >= c), 1.0, 0.0)
    upper = jnp.where(same & (r <= c), 1.0, 0.0)
    cum_f = jnp.dot(lower, g, precision=lax.Precision.HIGHEST, preferred_element_type=F32)
    cum_b = jnp.dot(upper, g, precision=lax.Precision.HIGHEST, preferred_element_type=F32)
    lane = lax.broadcasted_iota(jnp.int32, (TILE, 128), 1) % GB_LANES
    comb = jnp.where(lane < 2, beta, jnp.where(lane == 2, cum_f, jnp.where(lane == 3, cum_b, 0.0)))
    for h in range(N_HEADS):
        gb_ref[h] = comb[:, h * GB_LANES:(h + 1) * GB_LANES]


def _in_proj(x, mods, prew, win_p, dncw, rgcw, rgcb, alog_row, dtb_row, *, is_lat, mod_row0):
    bsz, t, _ = x.shape
    n_tiles = t // TILE
    n_tok = bsz * t
    hb = TILE // HALO
    tok = lambda b, i: (b * n_tiles + i, 0)
    head_tok = lambda b, i: (0, b * n_tiles + i, 0)
    kern = functools.partial(_in_proj_kernel, is_lat=is_lat, mod_row0=mod_row0, n_tiles=n_tiles)
    return pl.pallas_call(
        kern,
        grid=(bsz, n_tiles),
        in_specs=[
            pl.BlockSpec((1, HALO, D_MODEL), lambda b, i: (b, jnp.maximum(i * hb - 1, 0), 0)),
            pl.BlockSpec((1, TILE, D_MODEL), lambda b, i: (b, i, 0)),
            pl.BlockSpec((1, HALO, D_MODEL), lambda b, i: (b, jnp.minimum((i + 1) * hb, t // HALO - 1), 0)),
            _const_spec((8, 3 * D_MODEL)),
            _const_spec((1, D_MODEL)),
            _const_spec((D_MODEL, D_INP)),
            _const_spec((CONV_W, 3 * D_DN)),
            _const_spec((CONV_W, D_RG)),
            _const_spec((1, D_RG)),
            _const_spec((1, 128)),
            _const_spec((1, 128)),
        ],
        out_specs=[
            pl.BlockSpec((N_HEADS, TILE, HEAD_DIM), head_tok),
            pl.BlockSpec((N_HEADS, TILE, HEAD_DIM), head_tok),
            pl.BlockSpec((N_HEADS, TILE, HEAD_DIM), head_tok),
            pl.BlockSpec((TILE, D_DN), tok),
            pl.BlockSpec((TILE, D_RG), tok),
            pl.BlockSpec((TILE, D_RG), tok),
            pl.BlockSpec((N_HEADS, TILE, GB_LANES), head_tok),
        ],
        out_shape=[
            jax.ShapeDtypeStruct((N_HEADS, n_tok, HEAD_DIM), F32),
            jax.ShapeDtypeStruct((N_HEADS, n_tok, HEAD_DIM), F32),
            jax.ShapeDtypeStruct((N_HEADS, n_tok, HEAD_DIM), F32),
            jax.ShapeDtypeStruct((n_tok, D_DN), F32),
            jax.ShapeDtypeStruct((n_tok, D_RG), F32),
            jax.ShapeDtypeStruct((n_tok, D_RG), F32),
            jax.ShapeDtypeStruct((N_HEADS, n_tok, GB_LANES), F32),
        ],
        scratch_shapes=[
            pltpu.VMEM((TILE + 2 * HALO, D_MODEL), BF16),
            pltpu.VMEM((TILE + 2 * HALO, SEG), F32),
            pltpu.VMEM((GRID_W, D_MODEL // 2), F32),
        ],
        compiler_params=pltpu.CompilerParams(dimension_semantics=("arbitrary", "arbitrary"),
                                             vmem_limit_bytes=VMEM_LIMIT),
        name="in_proj_lat" if is_lat else "in_proj_ctx",
    )(x, x, x, mods, prew, win_p, dncw, rgcw, rgcb, alog_row, dtb_row)


C_EYE, C_BIAS_F, C_BIAS_B, C_STRICT_F, C_STRICT_B, C_LEVEL0 = 0, 1, 2, 3, 4, 5
N_LEVELS = int(math.log2(CHUNK))
N_CONSTS = C_LEVEL0 + N_LEVELS


def _delta_consts(cst):
    r = lax.broadcasted_iota(jnp.int32, (TILE, TILE), 0)
    c = lax.broadcasted_iota(jnp.int32, (TILE, TILE), 1)
    same = (r // CHUNK) == (c // CHUNK)
    cst[C_EYE] = jnp.where(r == c, 1.0, 0.0)
    cst[C_BIAS_F] = jnp.where(same & (r >= c), 0.0, NEG_BIG)
    cst[C_BIAS_B] = jnp.where(same & (r <= c), 0.0, NEG_BIG)
    cst[C_STRICT_F] = jnp.where(same & (r > c), 1.0, 0.0)
    cst[C_STRICT_B] = jnp.where(same & (r < c), 1.0, 0.0)
    for lv in range(N_LEVELS):
        m = 1 << lv
        cst[C_LEVEL0 + lv] = jnp.where(((r // (2 * m)) == (c // (2 * m))) & ((r // m) != (c // m)), 1.0, 0.0)


def _delta_dir(q, k, v, gb, s_ref, cst, d, reverse):
    beta = gb[:, d:d + 1]
    gcum = gb[:, 2 + d:3 + d]
    eye8 = jnp.where(lax.broadcasted_iota(jnp.int32, (GB_LANES, GB_LANES), 0)
                     == lax.broadcasted_iota(jnp.int32, (GB_LANES, GB_LANES), 1), 1.0, 0.0)
    gb_t = lax.dot_general(eye8, gb, (((1,), (1,)), ((), ())), precision=lax.Precision.HIGHEST,
                           preferred_element_type=F32)
    gcum_row = gb_t[2 + d:3 + d, :]
    decay = jnp.exp((gcum - gcum_row) + cst[C_BIAS_B if reverse else C_BIAS_F])

    qs = q * (HEAD_DIM ** -0.5)
    kbeta = k * beta
    prod = lax.dot_general(jnp.concatenate([qs, kbeta], axis=0).astype(BF16), k.astype(BF16),
                           (((1,), (1,)), ((), ())), preferred_element_type=F32)
    attn = prod[:TILE] * decay
    nmat = prod[TILE:] * decay * cst[C_STRICT_B if reverse else C_STRICT_F]

    tinv = cst[C_EYE] - nmat * cst[C_LEVEL0]
    for lv in range(1, N_LEVELS):
        tinv = tinv - _mm(_mm(tinv, nmat * cst[C_LEVEL0 + lv]), tinv)

    eg = jnp.exp(gcum)
    sol = _mm(tinv, jnp.concatenate([v * beta, kbeta * eg], axis=1))
    u = sol[:, :HEAD_DIM]
    w = sol[:, HEAD_DIM:]
    qd = qs * eg

    tot = []
    for ci in range(N_CHUNKS):
        row = ci * CHUNK if reverse else ci * CHUNK + CHUNK - 1
        tot.append(gcum[row:row + 1, :])
    gtot = jnp.concatenate([jnp.broadcast_to(t_, (CHUNK, 1)) for t_ in tot], axis=0)
    kd = k * jnp.exp(gtot - gcum)

    s = s_ref[d]
    vnew = [None] * N_CHUNKS
    obase = [None] * N_CHUNKS
    order = range(N_CHUNKS - 1, -1, -1) if reverse else range(N_CHUNKS)
    for ci in order:
        rows = slice(ci * CHUNK, (ci + 1) * CHUNK)
        ws = _mm(jnp.concatenate([w[rows], qd[rows]], axis=0), s)
        vn = u[rows] - ws[:CHUNK]
        obase[ci] = ws[CHUNK:]
        vnew[ci] = vn
        upd = lax.dot_general(kd[rows].astype(BF16), vn.astype(BF16), (((0,), (0,)), ((), ())),
                              preferred_element_type=F32)
        s = s * jnp.exp(tot[ci]) + upd
    s_ref[d] = s
    return jnp.concatenate(obase, axis=0) + _mm(attn, jnp.concatenate(vnew, axis=0))


def _delta_kernel(*refs, has_s0, emit_state, n_tiles):
    if has_s0:
        (qf, kf, vf, gbf, qb, kb, vb, gbb, s0_ref), rest = refs[:9], refs[9:]
    else:
        (qf, kf, vf, gbf, qb, kb, vb, gbb), rest = refs[:8], refs[8:]
        s0_ref = None
    if emit_state:
        of_ref, ob_ref, sout_ref, s_scr, cst = rest
    else:
        of_ref, ob_ref, s_scr, cst = rest
        sout_ref = None
    i = pl.program_id(2)

    @pl.when((pl.program_id(0) == 0) & (pl.program_id(1) == 0) & (i == 0))
    def _():
        _delta_consts(cst)

    @pl.when(i == 0)
    def _():
        if has_s0:
            s_scr[...] = s0_ref[0, :, 0]
        else:
            s_scr[...] = jnp.zeros_like(s_scr)

    of_ref[0] = _delta_dir(qf[0], kf[0], vf[0], gbf[0], s_scr, cst, 0, False)
    ob_ref[0] = _delta_dir(qb[0], kb[0], vb[0], gbb[0], s_scr, cst, 1, True)

    if emit_state:
        @pl.when(i == n_tiles - 1)
        def _():
            sout_ref[0, :, 0] = s_scr[...]


def _delta(q, k, v, gb, s0, *, bsz, n_tiles, emit_state):
    n_tok = bsz * n_tiles * TILE
    fwd = lambda b, h, i: (h, b * n_tiles + i, 0)
    bwd = lambda b, h, i: (h, b * n_tiles + n_tiles - 1 - i, 0)
    has_s0 = s0 is not None
    in_specs, args = [], []
    for m in (fwd, bwd):
        in_specs += [pl.BlockSpec((1, TILE, HEAD_DIM), m)] * 3 + [pl.BlockSpec((1, TILE, GB_LANES), m)]
        args += [q, k, v, gb]
    if has_s0:
        in_specs.append(pl.BlockSpec((1, 2, 1, HEAD_DIM, HEAD_DIM), lambda b, h, i: (b, 0, h, 0, 0)))
        args.append(s0)
    out_specs = [pl.BlockSpec((1, TILE, HEAD_DIM), fwd), pl.BlockSpec((1, TILE, HEAD_DIM), bwd)]
    out_shape = [jax.ShapeDtypeStruct((N_HEADS, n_tok, HEAD_DIM), F32)] * 2
    if emit_state:
        out_specs.append(pl.BlockSpec((1, 2, 1, HEAD_DIM, HEAD_DIM), lambda b, h, i: (b, 0, h, 0, 0)))
        out_shape.append(jax.ShapeDtypeStruct((bsz, 2, N_HEADS, HEAD_DIM, HEAD_DIM), F32))
    kern = functools.partial(_delta_kernel, has_s0=has_s0, emit_state=emit_state, n_tiles=n_tiles)
    return pl.pallas_call(
        kern,
        grid=(bsz, N_HEADS, n_tiles),
        in_specs=in_specs,
        out_specs=out_specs,
        out_shape=out_shape,
        scratch_shapes=[pltpu.VMEM((2, HEAD_DIM, HEAD_DIM), F32), pltpu.VMEM((N_CONSTS, TILE, TILE), F32)],
        compiler_params=pltpu.CompilerParams(dimension_semantics=("arbitrary", "arbitrary", "arbitrary"),
                                             vmem_limit_bytes=VMEM_LIMIT),
        name="delta_lat" if has_s0 else "delta_ctx",
    )(*args)


def _rglru_dir(xf_ref, w_ref, br_ref, bi_ref, lam_ref, carry_ref, a_scr, b_scr, h_ref, d, reverse):
    for n in range(N_RG_BLOCKS):
        cols = slice(n * RG_BLOCK, (n + 1) * RG_BLOCK)
        xb = xf_ref[:, cols]
        ri = jnp.dot(xb.astype(BF16), w_ref[d, n], preferred_element_type=F32)
        rg = _sigmoid(ri[:, :RG_BLOCK] + br_ref[d:d + 1, cols])
        ig = _sigmoid(ri[:, RG_BLOCK:] + bi_ref[d:d + 1, cols])
        log_a = -RG_C * rg * _softplus(-lam_ref[d:d + 1, cols])
        a_scr[:, cols] = jnp.exp(log_a)
        b_scr[:, cols] = jnp.sqrt(_neg_expm1(2.0 * log_a)) * (ig * xb)

    n_groups = TILE // 8
    sub = lax.broadcasted_iota(jnp.int32, (8, D_RG), 0)

    def body(gidx, hprev):
        gi = (n_groups - 1 - gidx) if reverse else gidx
        rows = pl.ds(pl.multiple_of(gi * 8, 8), 8)
        a = a_scr[rows, :]
        bx = b_scr[rows, :]
        for sh in (1, 2, 4):
            if reverse:
                valid = sub < 8 - sh
                a_sh = jnp.where(valid, pltpu.roll(a, 8 - sh, 0), 1.0)
                b_sh = jnp.where(valid, pltpu.roll(bx, 8 - sh, 0), 0.0)
            else:
                valid = sub >= sh
                a_sh = jnp.where(valid, pltpu.roll(a, sh, 0), 1.0)
                b_sh = jnp.where(valid, pltpu.roll(bx, sh, 0), 0.0)
            bx = bx + a * b_sh
            a = a * a_sh
        hh = bx + a * hprev
        h_ref[rows, :] = hh
        return hh[0:1, :] if reverse else hh[7:8, :]

    carry_ref[d:d + 1, :] = lax.fori_loop(0, n_groups, body, carry_ref[d:d + 1, :])


def _rglru_kernel(*refs, has_h0, emit_state, n_tiles):
    if has_h0:
        (xff, xfb, h0_ref), rest = refs[:3], refs[3:]
    else:
        (xff, xfb), rest = refs[:2], refs[2:]
        h0_ref = None
    w_ref, br_ref, bi_ref, lam_ref = rest[:4]
    rest = rest[4:]
    if emit_state:
        hf_ref, hb_ref, sfin_ref, carry, a_scr, b_scr = rest
    else:
        hf_ref, hb_ref, carry, a_scr, b_scr = rest
        sfin_ref = None
    i = pl.program_id(1)

    @pl.when(i == 0)
    def _():
        if has_h0:
            carry[...] = h0_ref[0]
        else:
            carry[...] = jnp.zeros_like(carry)

    _rglru_dir(xff, w_ref, br_ref, bi_ref, lam_ref, carry, a_scr, b_scr, hf_ref, 0, False)
    _rglru_dir(xfb, w_ref, br_ref, bi_ref, lam_ref, carry, a_scr, b_scr, hb_ref, 1, True)

    if emit_state:
        @pl.when(i == n_tiles - 1)
        def _():
            sfin_ref[0] = carry[...]


def _rglru(xf, h0, wcat, br, bi, lam, *, bsz, n_tiles, emit_state):
    n_tok = bsz * n_tiles * TILE
    fwd = lambda b, i: (b * n_tiles + i, 0)
    bwd = lambda b, i: (b * n_tiles + n_tiles - 1 - i, 0)
    has_h0 = h0 is not None
    in_specs = [pl.BlockSpec((TILE, D_RG), fwd), pl.BlockSpec((TILE, D_RG), bwd)]
    args = [xf, xf]
    if has_h0:
        in_specs.append(pl.BlockSpec((1, 2, D_RG), lambda b, i: (b, 0, 0)))
        args.append(h0)
    in_specs += [_const_spec((2, N_RG_BLOCKS, RG_BLOCK, 2 * RG_BLOCK)), _const_spec((2, D_RG)),
                 _const_spec((2, D_RG)), _const_spec((2, D_RG))]
    args += [wcat, br, bi, lam]
    out_specs = [pl.BlockSpec((TILE, D_RG), fwd), pl.BlockSpec((TILE, D_RG), bwd)]
    out_shape = [jax.ShapeDtypeStruct((n_tok, D_RG), F32)] * 2
    if emit_state:
        out_specs.append(pl.BlockSpec((1, 2, D_RG), lambda b, i: (b, 0, 0)))
        out_shape.append(jax.ShapeDtypeStruct((bsz, 2, D_RG), F32))
    kern = functools.partial(_rglru_kernel, has_h0=has_h0, emit_state=emit_state, n_tiles=n_tiles)
    return pl.pallas_call(
        kern,
        grid=(bsz, n_tiles),
        in_specs=in_specs,
        out_specs=out_specs,
        out_shape=out_shape,
        scratch_shapes=[pltpu.VMEM((2, D_RG), F32), pltpu.VMEM((TILE, D_RG), F32), pltpu.VMEM((TILE, D_RG), F32)],
        compiler_params=pltpu.CompilerParams(dimension_semantics=("arbitrary", "arbitrary"),
                                             vmem_limit_bytes=VMEM_LIMIT),
        name="rglru_lat" if has_h0 else "rglru_ctx",
    )(*args)


def _out_proj_kernel(x_ref, of_ref, ob_ref, z_ref, hf_ref, hb_ref, gate_ref, mods_ref, dnw_ref, postw_ref,
                     wout_ref, y_ref, mix_scr, tab_scr, *, is_lat, mod_row0):
    b = pl.program_id(0)
    i = pl.program_id(1)
    if is_lat:
        @pl.when((b == 0) & (i == 0))
        def _():
            tab_scr[...] = _pe_table()

    dnw = dnw_ref[...]
    for h in range(N_HEADS):
        cols = slice(h * HEAD_DIM, (h + 1) * HEAD_DIM)
        o = of_ref[h] + ob_ref[h]
        on = o * lax.rsqrt(jnp.mean(o * o, axis=-1, keepdims=True) + EPS) * dnw
        mix_scr[:, cols] = (on * _silu(z_ref[:, cols])).astype(BF16)
    mix_scr[:, D_DN:] = ((hf_ref[...] + hb_ref[...]) * _silu(gate_ref[...])).astype(BF16)

    y = jnp.dot(mix_scr[...], wout_ref[...], preferred_element_type=F32)
    yn = y * lax.rsqrt(jnp.mean(y * y, axis=-1, keepdims=True) + EPS) * postw_ref[...]
    mod = mods_ref[pl.ds(mod_row0 + (b if is_lat else 0), 1), :]
    xs = x_ref[0]
    if is_lat:
        xs = xs + _pe_tile(tab_scr, i)
    y_ref[0] = xs + mod[:, 2 * D_MODEL:] * yn


def _out_proj(x, o_f, o_b, z, h_f, h_b, gate, mods, dnw, postw, wout, *, is_lat, mod_row0):
    bsz, t, _ = x.shape
    n_tiles = t // TILE
    tok = lambda b, i: (b * n_tiles + i, 0)
    head_tok = lambda b, i: (0, b * n_tiles + i, 0)
    kern = functools.partial(_out_proj_kernel, is_lat=is_lat, mod_row0=mod_row0)
    return pl.pallas_call(
        kern,
        grid=(bsz, n_tiles),
        in_specs=[
            pl.BlockSpec((1, TILE, D_MODEL), lambda b, i: (b, i, 0)),
            pl.BlockSpec((N_HEADS, TILE, HEAD_DIM), head_tok),
            pl.BlockSpec((N_HEADS, TILE, HEAD_DIM), head_tok),
            pl.BlockSpec((TILE, D_DN), tok),
            pl.BlockSpec((TILE, D_RG), tok),
            pl.BlockSpec((TILE, D_RG), tok),
            pl.BlockSpec((TILE, D_RG), tok),
            _const_spec((8, 3 * D_MODEL)),
            _const_spec((1, HEAD_DIM)),
            _const_spec((1, D_MODEL)),
            _const_spec((D_DN + D_RG, D_MODEL)),
        ],
        out_specs=pl.BlockSpec((1, TILE, D_MODEL), lambda b, i: (b, i, 0)),
        out_shape=jax.ShapeDtypeStruct((bsz, t, D_MODEL), F32),
        scratch_shapes=[pltpu.VMEM((TILE, D_DN + D_RG), BF16), pltpu.VMEM((GRID_W, D_MODEL // 2), F32)],
        compiler_params=pltpu.CompilerParams(dimension_semantics=("arbitrary", "arbitrary"),
                                             vmem_limit_bytes=VMEM_LIMIT),
        name="out_proj_lat" if is_lat else "out_proj_ctx",
    )(x, o_f, o_b, z, h_f, h_b, gate, mods, dnw, postw, wout)


def _permute_small(w_cols):
    lead = w_cols.shape[:-1]
    ba = w_cols.reshape(lead + (2, 2, N_HEADS))
    per_head = jnp.moveaxis(ba, -1, -3).reshape(lead + (N_HEADS, 4))
    per_head = jnp.pad(per_head, [(0, 0)] * len(lead) + [(0, 0), (0, GB_LANES - 4)])
    flat = per_head.reshape(lead + (N_HEADS * GB_LANES,))
    return jnp.pad(flat, [(0, 0)] * len(lead) + [(0, 128 - N_HEADS * GB_LANES)])


def _mixer_path(x, mods, weights, s_dn0, s_rg0, *, is_lat, mod_row0):
    (prew, postw, win_p, wout, dncw, alog_row, dtb_row, dnw, rgcw, rgcb, wcat, br, bi, lam) = weights
    bsz, t, _ = x.shape
    n_tiles = t // TILE
    q, k, v, z, xf, gate, gb = _in_proj(x, mods, prew, win_p, dncw, rgcw, rgcb, alog_row, dtb_row,
                                        is_lat=is_lat, mod_row0=mod_row0)
    emit = not is_lat
    dn = _delta(q, k, v, gb, s_dn0, bsz=bsz, n_tiles=n_tiles, emit_state=emit)
    rg = _rglru(xf, s_rg0, wcat, br, bi, lam, bsz=bsz, n_tiles=n_tiles, emit_state=emit)
    y = _out_proj(x, dn[0], dn[1], z, rg[0], rg[1], gate, mods, dnw, postw, wout,
                  is_lat=is_lat, mod_row0=mod_row0)
    if emit:
        return y, dn[2], rg[2]
    return y


def kernel(x_prompt, x_sample, state_delta, state_rglru, c, c_ctx, ada_w, ada_b, pre_norm_w, post_norm_w, w_in, w_out, dn_conv_w, dn_a_log, dn_dt_bias, dn_norm_w, rg_conv_w, rg_conv_b, rg_w_r, rg_b_r, rg_w_i, rg_b_i, rg_lam):
    l = 0
    n_lat = c.shape[0]
    c_all = jnp.concatenate([c_ctx[None], c, jnp.zeros((8 - 1 - n_lat, D_MODEL), F32)], axis=0)
    mods = _mods(c_all, ada_w[l].astype(BF16), ada_b[l][None])

    w = w_in[l]
    win_p = jnp.concatenate([w[:, :OFF_B], w[:, OFF_RX:], _permute_small(w[:, OFF_B:OFF_RX])], axis=1).astype(BF16)
    zeros16 = jnp.zeros((2 * N_HEADS,), F32)
    alog_row = _permute_small(jnp.concatenate([zeros16, dn_a_log[l].reshape(-1)]))[None]
    dtb_row = _permute_small(jnp.concatenate([zeros16, dn_dt_bias[l].reshape(-1)]))[None]
    wcat = jnp.concatenate([rg_w_r[l], rg_w_i[l]], axis=-1).astype(BF16)
    weights = (pre_norm_w[l][None], post_norm_w[l][None], win_p, w_out[l].astype(BF16), dn_conv_w[l],
               alog_row, dtb_row, dn_norm_w[l][None], rg_conv_w[l], rg_conv_b[l][None], wcat,
               rg_b_r[l], rg_b_i[l], rg_lam[l])

    y_prompt, s_dn, s_rg = _mixer_path(x_prompt, mods, weights, None, None, is_lat=False, mod_row0=0)
    y_sample = _mixer_path(x_sample, mods, weights, state_delta[:, l], state_rglru[:, l], is_lat=True, mod_row0=1)
    return (y_prompt, y_sample, s_dn[:, None].astype(x_prompt.dtype), s_rg[:, None].astype(x_prompt.dtype))
```

```python
import functools
import math

import jax
import jax.numpy as jnp
from jax import lax
from jax.experimental import pallas as pl
from jax.experimental.pallas import tpu as pltpu

F32 = jnp.float32
BF16 = jnp.bfloat16

D_MODEL = 1024
N_HEADS = 8
HEAD_DIM = 128
D_DN = N_HEADS * HEAD_DIM
D_RG = 1024
N_RG_BLOCKS = 8
RG_BLOCK = D_RG // N_RG_BLOCKS
CONV_W = 4
CONV_PAD_L = 2
CHUNK = 64
RG_C = 8.0
EPS = 1e-6
GRID_W = 64
OFF_Z = 3 * D_DN
OFF_B = 4 * D_DN
OFF_A = OFF_B + 2 * N_HEADS
OFF_RX = OFF_A + 2 * N_HEADS
D_IN = OFF_RX + 2 * D_RG

TILE = 256
HALO = 16
N_CHUNKS = TILE // CHUNK
SEG = 512
GB_LANES = 8
DELTA_HEADS = 4
P_Z = 3 * D_DN
P_RX = P_Z + D_DN
P_RG = P_RX + D_RG
P_SMALL = P_RG + D_RG
D_INP = P_SMALL + 128
NEG_BIG = -1e30
VMEM_LIMIT = 56 * 1024 * 1024


def _sigmoid(x):
    return 0.5 * jnp.tanh(0.5 * x) + 0.5


def _neg_expm1(y):
    t = jnp.tanh(0.5 * y)
    return -2.0 * t / (1.0 - t)


def _silu(x):
    return x * _sigmoid(x)


def _softplus(x):
    return jnp.maximum(x, 0.0) + jnp.log1p(jnp.exp(-jnp.abs(x)))


def _mm(a, b):
    return jnp.dot(a.astype(BF16), b.astype(BF16), preferred_element_type=F32)


def _const_spec(shape):
    nd = len(shape)
    return pl.BlockSpec(shape, lambda *_: (0,) * nd, pipeline_mode=pl.Buffered(1))


def _pe_table():
    quarter = D_MODEL // 4
    j = lax.broadcasted_iota(jnp.int32, (GRID_W, quarter), 0).astype(F32)
    kf = lax.broadcasted_iota(jnp.int32, (GRID_W, quarter), 1).astype(F32)
    freqs = jnp.exp(-math.log(10000.0) * kf / quarter)
    ang = j * freqs
    return jnp.concatenate([jnp.sin(ang), jnp.cos(ang)], axis=1)


def _pe_rows(tab_ref, grid_row, n_rows, col0):
    row_part = jnp.broadcast_to(tab_ref[pl.ds(grid_row, 1), :], (n_rows, D_MODEL // 2))
    col_part = tab_ref[col0:col0 + n_rows, :]
    return jnp.concatenate([row_part, col_part], axis=1)


def _pe_tile(tab_ref, i):
    r0 = i * (TILE // GRID_W)
    return jnp.concatenate([_pe_rows(tab_ref, r0 + j, GRID_W, 0) for j in range(TILE // GRID_W)], axis=0)


def _mods_kernel(c_ref, w_ref, b_ref, o_ref):
    o_ref[...] = _mm(_silu(c_ref[...]), w_ref[...]) + b_ref[...]


def _mods(c_all, ada_w, ada_b):
    return pl.pallas_call(
        _mods_kernel,
        grid=(3,),
        in_specs=[
            pl.BlockSpec((8, D_MODEL), lambda j: (0, 0)),
            pl.BlockSpec((D_MODEL, D_MODEL), lambda j: (0, j)),
            pl.BlockSpec((1, D_MODEL), lambda j: (0, j)),
        ],
        out_specs=pl.BlockSpec((8, D_MODEL), lambda j: (0, j)),
        out_shape=jax.ShapeDtypeStruct((8, 3 * D_MODEL), F32),
        compiler_params=pltpu.CompilerParams(dimension_semantics=("arbitrary",), vmem_limit_bytes=VMEM_LIMIT),
        name="mods",
    )(c_all, ada_w, ada_b)


def _in_proj_kernel(xp_ref, x_ref, xn_ref, mods_ref, prew_ref, win_ref, dncw_ref, rgcw_ref, rgcb_ref,
                    alog_ref, dtb_ref,
                    q_ref, k_ref, v_ref, z_ref, xf_ref, gate_ref, gb_ref,
                    h_scr, p_scr, tab_scr, *, is_lat, mod_row0, n_tiles):
    b = pl.program_id(0)
    i = pl.program_id(1)

    if is_lat:
        @pl.when((b == 0) & (i == 0))
        def _():
            tab_scr[...] = _pe_table()

    mod = mods_ref[pl.ds(mod_row0 + (b if is_lat else 0), 1), :]
    shift = mod[:, :D_MODEL]
    scale = mod[:, D_MODEL:2 * D_MODEL]
    prew = prew_ref[...]

    def norm_mod(xv):
        ms = jnp.mean(xv * xv, axis=-1, keepdims=True)
        return (xv * lax.rsqrt(ms + EPS) * prew) * (1.0 + scale) + shift

    x_prev = xp_ref[0]
    x_main = x_ref[0]
    x_next = xn_ref[0]
    if is_lat:
        r0 = i * (TILE // GRID_W)
        x_main = x_main + _pe_tile(tab_scr, i)
        x_prev = x_prev + _pe_rows(tab_scr, jnp.maximum(r0 - 1, 0), HALO, GRID_W - HALO)
        x_next = x_next + _pe_rows(tab_scr, jnp.minimum(r0 + TILE // GRID_W, GRID_W - 1), HALO, 0)

    h_scr[0:HALO, :] = jnp.where(i > 0, norm_mod(x_prev), 0.0).astype(BF16)
    h_scr[HALO:HALO + TILE, :] = norm_mod(x_main).astype(BF16)
    h_scr[HALO + TILE:, :] = jnp.where(i < n_tiles - 1, norm_mod(x_next), 0.0).astype(BF16)

    def conv_seg(col0, cw_ref, cw_col0):
        p_scr[...] = jnp.dot(h_scr[...], win_ref[:, col0:col0 + SEG], preferred_element_type=F32)
        acc = None
        for j in range(CONV_W):
            tap = p_scr[pl.ds(HALO - CONV_PAD_L + j, TILE), :] * cw_ref[j:j + 1, cw_col0:cw_col0 + SEG]
            acc = tap if acc is None else acc + tap
        return acc

    def plain_seg(col0, width):
        return jnp.dot(h_scr[HALO:HALO + TILE, :], win_ref[:, col0:col0 + width], preferred_element_type=F32)

    heads_per_seg = SEG // HEAD_DIM
    for s in range(3 * D_DN // SEG):
        act = _silu(conv_seg(s * SEG, dncw_ref, s * SEG))
        which = (s * SEG) // D_DN
        out_ref = (q_ref, k_ref, v_ref)[which]
        for hl in range(heads_per_seg):
            head = ((s * SEG) % D_DN) // HEAD_DIM + hl
            xh = act[:, hl * HEAD_DIM:(hl + 1) * HEAD_DIM]
            if which < 2:
                xh = xh * lax.rsqrt(jnp.sum(xh * xh, axis=-1, keepdims=True) + EPS)
            out_ref[head] = xh

    for s in range(D_DN // SEG):
        z_ref[:, s * SEG:(s + 1) * SEG] = plain_seg(P_Z + s * SEG, SEG)
    for s in range(D_RG // SEG):
        xf_ref[:, s * SEG:(s + 1) * SEG] = (conv_seg(P_RX + s * SEG, rgcw_ref, s * SEG)
                                            + rgcb_ref[:, s * SEG:(s + 1) * SEG])
    for s in range(D_RG // SEG):
        gate_ref[:, s * SEG:(s + 1) * SEG] = plain_seg(P_RG + s * SEG, SEG)

    raw = plain_seg(P_SMALL, 128)
    beta = _sigmoid(raw)
    g = -jnp.exp(alog_ref[...]) * _softplus(raw + dtb_ref[...])
    r = lax.broadcasted_iota(jnp.int32, (TILE, TILE), 0)
    c = lax.broadcasted_iota(jnp.int32, (TILE, TILE), 1)
    same = (r // CHUNK) == (c // CHUNK)
    lower = jnp.where(same & (r >= c), 1.0, 0.0)
    upper = jnp.where(same & (r <= c), 1.0, 0.0)
    cum_f = jnp.dot(lower, g, precision=lax.Precision.HIGHEST, preferred_element_type=F32)
    cum_b = jnp.dot(upper, g, precision=lax.Precision.HIGHEST, preferred_element_type=F32)
    lane = lax.broadcasted_iota(jnp.int32, (TILE, 128), 1) % GB_LANES
    comb = jnp.where(lane < 2, beta, jnp.where(lane == 2, cum_f, jnp.where(lane == 3, cum_b, 0.0)))
    for h in range(N_HEADS):
        gb_ref[h] = comb[:, h * GB_LANES:(h + 1) * GB_LANES]


def _in_proj(x, mods, prew, win_p, dncw, rgcw, rgcb, alog_row, dtb_row, *, is_lat, mod_row0):
    bsz, t, _ = x.shape
    n_tiles = t // TILE
    n_tok = bsz * t
    hb = TILE // HALO
    tok = lambda b, i: (b * n_tiles + i, 0)
    head_tok = lambda b, i: (0, b * n_tiles + i, 0)
    kern = functools.partial(_in_proj_kernel, is_lat=is_lat, mod_row0=mod_row0, n_tiles=n_tiles)
    return pl.pallas_call(
        kern,
        grid=(bsz, n_tiles),
        in_specs=[
            pl.BlockSpec((1, HALO, D_MODEL), lambda b, i: (b, jnp.maximum(i * hb - 1, 0), 0)),
            pl.BlockSpec((1, TILE, D_MODEL), lambda b, i: (b, i, 0)),
            pl.BlockSpec((1, HALO, D_MODEL), lambda b, i: (b, jnp.minimum((i + 1) * hb, t // HALO - 1), 0)),
            _const_spec((8, 3 * D_MODEL)),
            _const_spec((1, D_MODEL)),
            _const_spec((D_MODEL, D_INP)),
            _const_spec((CONV_W, 3 * D_DN)),
            _const_spec((CONV_W, D_RG)),
            _const_spec((1, D_RG)),
            _const_spec((1, 128)),
            _const_spec((1, 128)),
        ],
        out_specs=[
            pl.BlockSpec((N_HEADS, TILE, HEAD_DIM), head_tok),
            pl.BlockSpec((N_HEADS, TILE, HEAD_DIM), head_tok),
            pl.BlockSpec((N_HEADS, TILE, HEAD_DIM), head_tok),
            pl.BlockSpec((TILE, D_DN), tok),
            pl.BlockSpec((TILE, D_RG), tok),
            pl.BlockSpec((TILE, D_RG), tok),
            pl.BlockSpec((N_HEADS, TILE, GB_LANES), head_tok),
        ],
        out_shape=[
            jax.ShapeDtypeStruct((N_HEADS, n_tok, HEAD_DIM), F32),
            jax.ShapeDtypeStruct((N_HEADS, n_tok, HEAD_DIM), F32),
            jax.ShapeDtypeStruct((N_HEADS, n_tok, HEAD_DIM), F32),
            jax.ShapeDtypeStruct((n_tok, D_DN), F32),
            jax.ShapeDtypeStruct((n_tok, D_RG), F32),
            jax.ShapeDtypeStruct((n_tok, D_RG), F32),
            jax.ShapeDtypeStruct((N_HEADS, n_tok, GB_LANES), F32),
        ],
        scratch_shapes=[
            pltpu.VMEM((TILE + 2 * HALO, D_MODEL), BF16),
            pltpu.VMEM((TILE + 2 * HALO, SEG), F32),
            pltpu.VMEM((GRID_W, D_MODEL // 2), F32),
        ],
        compiler_params=pltpu.CompilerParams(dimension_semantics=("arbitrary", "arbitrary"),
                                             vmem_limit_bytes=VMEM_LIMIT),
        name="in_proj_lat" if is_lat else "in_proj_ctx",
    )(x, x, x, mods, prew, win_p, dncw, rgcw, rgcb, alog_row, dtb_row)


K_EYE, K_INCL_F, K_INCL_B, K_STRICT_F, K_STRICT_B, K_LEVEL0 = 0, 1, 2, 3, 4, 5
N_LEVELS = int(math.log2(CHUNK))
N_CAT_CONSTS = K_LEVEL0 + N_LEVELS


def _delta_consts(cst, half):
    r = lax.broadcasted_iota(jnp.int32, (CHUNK, TILE), 0)
    c = lax.broadcasted_iota(jnp.int32, (CHUNK, TILE), 1) % CHUNK
    cst[K_EYE] = jnp.where(r == c, 1.0, 0.0)
    cst[K_INCL_F] = jnp.where(r >= c, 0.0, NEG_BIG)
    cst[K_INCL_B] = jnp.where(r <= c, 0.0, NEG_BIG)
    cst[K_STRICT_F] = jnp.where(r > c, 0.0, NEG_BIG)
    cst[K_STRICT_B] = jnp.where(r < c, 0.0, NEG_BIG)
    for lv in range(N_LEVELS):
        m = 1 << lv
        cst[K_LEVEL0 + lv] = jnp.where(((r // (2 * m)) == (c // (2 * m))) & ((r // m) != (c // m)), 1.0, 0.0)
    lane = lax.broadcasted_iota(jnp.int32, (CHUNK, 128), 1)
    half[0] = jnp.where(lane < CHUNK, 1.0, 0.0).astype(BF16)
    half[1] = jnp.where(lane >= CHUNK, 1.0, 0.0).astype(BF16)


def _cat_of_diag_blocks(full):
    lo = lax.broadcasted_iota(jnp.int32, (CHUNK, 128), 1) < CHUNK
    tiles = []
    for t in range(TILE // 128):
        a = full[(2 * t) * CHUNK:(2 * t + 1) * CHUNK, t * 128:(t + 1) * 128]
        b = full[(2 * t + 1) * CHUNK:(2 * t + 2) * CHUNK, t * 128:(t + 1) * 128]
        tiles.append(jnp.where(lo, a, b))
    return jnp.concatenate(tiles, axis=1)


def _cat_of_columns(col):
    lo = lax.broadcasted_iota(jnp.int32, (CHUNK, 128), 1) < CHUNK
    tiles = []
    for t in range(TILE // 128):
        a = jnp.broadcast_to(col[(2 * t) * CHUNK:(2 * t + 1) * CHUNK, :], (CHUNK, 128))
        b = jnp.broadcast_to(col[(2 * t + 1) * CHUNK:(2 * t + 2) * CHUNK, :], (CHUNK, 128))
        tiles.append(jnp.where(lo, a, b))
    return jnp.concatenate(tiles, axis=1)


def _block_diag(cat, half):
    zero = jnp.zeros((CHUNK, 128), BF16)
    rows = []
    for j in range(N_CHUNKS):
        t = j // 2
        blk = cat[:, t * 128:(t + 1) * 128] * half[j % 2]
        rows.append(jnp.concatenate([blk if tt == t else zero for tt in range(TILE // 128)], axis=1))
    return jnp.concatenate(rows, axis=0)


def _delta_tile(probs, s_ref, cst, half):
    n = len(probs)
    eye8 = jnp.where(lax.broadcasted_iota(jnp.int32, (GB_LANES, GB_LANES), 0)
                     == lax.broadcasted_iota(jnp.int32, (GB_LANES, GB_LANES), 1), 1.0, 0.0)
    beta, gcum, qs, kbeta, attn, ncat = [], [], [], [], [], []
    for q, k, v, gb, hl, d in probs:
        beta.append(gb[:, d:d + 1])
        gcum.append(gb[:, 2 + d:3 + d])
        gb_t = lax.dot_general(eye8, gb, (((1,), (1,)), ((), ())), precision=lax.Precision.HIGHEST,
                               preferred_element_type=F32)
        diff = _cat_of_columns(gcum[-1]) - gb_t[2 + d:3 + d, :]
        qs.append(q * (HEAD_DIM ** -0.5))
        kbeta.append(k * beta[-1])
        prod = lax.dot_general(jnp.concatenate([qs[-1], kbeta[-1]], axis=0).astype(BF16), k.astype(BF16),
                               (((1,), (1,)), ((), ())), preferred_element_type=F32)
        attn.append(_cat_of_diag_blocks(prod[:TILE]) * jnp.exp(diff + cst[K_INCL_B if d else K_INCL_F]))
        ncat.append(_cat_of_diag_blocks(prod[TILE:]) * jnp.exp(diff + cst[K_STRICT_B if d else K_STRICT_F]))

    xcat = [cst[K_EYE] - ncat[p] * cst[K_LEVEL0] for p in range(n)]
    xbd = [_block_diag(xcat[p].astype(BF16), half) for p in range(n)]
    for lv in range(1, N_LEVELS):
        cbd = [_block_diag((ncat[p] * cst[K_LEVEL0 + lv]).astype(BF16), half) for p in range(n)]
        xc = [jnp.dot(xcat[p].astype(BF16), cbd[p], preferred_element_type=F32) for p in range(n)]
        xcat = [xcat[p] - jnp.dot(xc[p].astype(BF16), xbd[p], preferred_element_type=F32) for p in range(n)]
        xbd = [_block_diag(xcat[p].astype(BF16), half) for p in range(n)]

    u, w, qd, kd, tot = [], [], [], [], []
    for p, (q, k, v, gb, hl, d) in enumerate(probs):
        eg = jnp.exp(gcum[p])
        rhs = jnp.concatenate([v * beta[p], kbeta[p] * eg], axis=1).astype(BF16)
        sol = jnp.dot(xbd[p], rhs, preferred_element_type=F32)
        u.append(sol[:, :HEAD_DIM])
        w.append(sol[:, HEAD_DIM:])
        qd.append(qs[p] * eg)
        last = [ci * CHUNK if d else ci * CHUNK + CHUNK - 1 for ci in range(N_CHUNKS)]
        tot.append([gcum[p][r0:r0 + 1, :] for r0 in last])
        gtot = jnp.concatenate([jnp.broadcast_to(x, (CHUNK, 1)) for x in tot[p]], axis=0)
        kd.append(k * jnp.exp(gtot - gcum[p]))

    s = [s_ref[d, hl] for (_, _, _, _, hl, d) in probs]
    vnew = [[None] * N_CHUNKS for _ in range(n)]
    obase = [[None] * N_CHUNKS for _ in range(n)]
    for step in range(N_CHUNKS):
        for p, (_, _, _, _, hl, d) in enumerate(probs):
            ci = N_CHUNKS - 1 - step if d else step
            rows = slice(ci * CHUNK, (ci + 1) * CHUNK)
            ws = _mm(jnp.concatenate([w[p][rows], qd[p][rows]], axis=0), s[p])
            vn = u[p][rows] - ws[:CHUNK]
            obase[p][ci] = ws[CHUNK:]
            vnew[p][ci] = vn.astype(BF16)
            upd = lax.dot_general(kd[p][rows].astype(BF16), vnew[p][ci], (((0,), (0,)), ((), ())),
                                  preferred_element_type=F32)
            s[p] = s[p] * jnp.exp(tot[p][ci]) + upd
    outs = []
    zero = jnp.zeros((CHUNK, HEAD_DIM), BF16)
    for p, (_, _, _, _, hl, d) in enumerate(probs):
        s_ref[d, hl] = s[p]
        vbd = jnp.concatenate([jnp.concatenate([vnew[p][j] if jj == j else zero for jj in range(N_CHUNKS)], axis=1)
                               for j in range(N_CHUNKS)], axis=0)
        ocat = jnp.dot(attn[p].astype(BF16), vbd, preferred_element_type=F32)
        outs.append(jnp.concatenate([obase[p][j] + ocat[:, j * HEAD_DIM:(j + 1) * HEAD_DIM]
                                     for j in range(N_CHUNKS)], axis=0))
    return outs


def _delta_kernel(*refs, has_s0, emit_state, n_tiles):
    if has_s0:
        (qf, kf, vf, gbf, qb, kb, vb, gbb, s0_ref), rest = refs[:9], refs[9:]
    else:
        (qf, kf, vf, gbf, qb, kb, vb, gbb), rest = refs[:8], refs[8:]
        s0_ref = None
    if emit_state:
        of_ref, ob_ref, sout_ref, s_scr, cst, half = rest
    else:
        of_ref, ob_ref, s_scr, cst, half = rest
        sout_ref = None
    i = pl.program_id(2)

    @pl.when((pl.program_id(0) == 0) & (pl.program_id(1) == 0) & (i == 0))
    def _():
        _delta_consts(cst, half)

    @pl.when(i == 0)
    def _():
        if has_s0:
            s_scr[...] = s0_ref[0]
        else:
            s_scr[...] = jnp.zeros_like(s_scr)

    probs = []
    for hl in range(DELTA_HEADS):
        probs.append((qf[hl], kf[hl], vf[hl], gbf[hl], hl, 0))
        probs.append((qb[hl], kb[hl], vb[hl], gbb[hl], hl, 1))
    outs = _delta_tile(probs, s_scr, cst, half)
    for hl in range(DELTA_HEADS):
        of_ref[hl] = outs[2 * hl]
        ob_ref[hl] = outs[2 * hl + 1]

    if emit_state:
        @pl.when(i == n_tiles - 1)
        def _():
            sout_ref[0] = s_scr[...]


def _delta(q, k, v, gb, s0, *, bsz, n_tiles, emit_state):
    n_tok = bsz * n_tiles * TILE
    hb = DELTA_HEADS
    fwd = lambda b, h, i: (h, b * n_tiles + i, 0)
    bwd = lambda b, h, i: (h, b * n_tiles + n_tiles - 1 - i, 0)
    has_s0 = s0 is not None
    in_specs, args = [], []
    for m in (fwd, bwd):
        in_specs += [pl.BlockSpec((hb, TILE, HEAD_DIM), m)] * 3 + [pl.BlockSpec((hb, TILE, GB_LANES), m)]
        args += [q, k, v, gb]
    state_spec = pl.BlockSpec((1, 2, hb, HEAD_DIM, HEAD_DIM), lambda b, h, i: (b, 0, h, 0, 0))
    if has_s0:
        in_specs.append(state_spec)
        args.append(s0)
    out_specs = [pl.BlockSpec((hb, TILE, HEAD_DIM), fwd), pl.BlockSpec((hb, TILE, HEAD_DIM), bwd)]
    out_shape = [jax.ShapeDtypeStruct((N_HEADS, n_tok, HEAD_DIM), F32)] * 2
    if emit_state:
        out_specs.append(state_spec)
        out_shape.append(jax.ShapeDtypeStruct((bsz, 2, N_HEADS, HEAD_DIM, HEAD_DIM), F32))
    kern = functools.partial(_delta_kernel, has_s0=has_s0, emit_state=emit_state, n_tiles=n_tiles)
    return pl.pallas_call(
        kern,
        grid=(bsz, N_HEADS // hb, n_tiles),
        in_specs=in_specs,
        out_specs=out_specs,
        out_shape=out_shape,
        scratch_shapes=[pltpu.VMEM((2, hb, HEAD_DIM, HEAD_DIM), F32), pltpu.VMEM((N_CAT_CONSTS, CHUNK, TILE), F32),
                        pltpu.VMEM((2, CHUNK, 128), BF16)],
        compiler_params=pltpu.CompilerParams(dimension_semantics=("arbitrary", "arbitrary", "arbitrary"),
                                             vmem_limit_bytes=VMEM_LIMIT),
        name="delta_lat" if has_s0 else "delta_ctx",
    )(*args)


def _rglru_dir(xf_ref, w_ref, br_ref, bi_ref, lam_ref, carry_ref, a_scr, b_scr, h_ref, d, reverse):
    for n in range(N_RG_BLOCKS):
        cols = slice(n * RG_BLOCK, (n + 1) * RG_BLOCK)
        xb = xf_ref[:, cols]
        ri = jnp.dot(xb.astype(BF16), w_ref[d, n], preferred_element_type=F32)
        rg = _sigmoid(ri[:, :RG_BLOCK] + br_ref[d:d + 1, cols])
        ig = _sigmoid(ri[:, RG_BLOCK:] + bi_ref[d:d + 1, cols])
        log_a = -RG_C * rg * _softplus(-lam_ref[d:d + 1, cols])
        a_scr[:, cols] = jnp.exp(log_a)
        b_scr[:, cols] = jnp.sqrt(_neg_expm1(2.0 * log_a)) * (ig * xb)

    n_groups = TILE // 8
    sub = lax.broadcasted_iota(jnp.int32, (8, D_RG), 0)

    def body(gidx, hprev):
        gi = (n_groups - 1 - gidx) if reverse else gidx
        rows = pl.ds(pl.multiple_of(gi * 8, 8), 8)
        a = a_scr[rows, :]
        bx = b_scr[rows, :]
        for sh in (1, 2, 4):
            if reverse:
                valid = sub < 8 - sh
                a_sh = jnp.where(valid, pltpu.roll(a, 8 - sh, 0), 1.0)
                b_sh = jnp.where(valid, pltpu.roll(bx, 8 - sh, 0), 0.0)
            else:
                valid = sub >= sh
                a_sh = jnp.where(valid, pltpu.roll(a, sh, 0), 1.0)
                b_sh = jnp.where(valid, pltpu.roll(bx, sh, 0), 0.0)
            bx = bx + a * b_sh
            a = a * a_sh
        hh = bx + a * hprev
        h_ref[rows, :] = hh
        return hh[0:1, :] if reverse else hh[7:8, :]

    carry_ref[d:d + 1, :] = lax.fori_loop(0, n_groups, body, carry_ref[d:d + 1, :])


def _rglru_kernel(*refs, has_h0, emit_state, n_tiles):
    if has_h0:
        (xff, xfb, h0_ref), rest = refs[:3], refs[3:]
    else:
        (xff, xfb), rest = refs[:2], refs[2:]
        h0_ref = None
    w_ref, br_ref, bi_ref, lam_ref = rest[:4]
    rest = rest[4:]
    if emit_state:
        hf_ref, hb_ref, sfin_ref, carry, a_scr, b_scr = rest
    else:
        hf_ref, hb_ref, carry, a_scr, b_scr = rest
        sfin_ref = None
    i = pl.program_id(1)

    @pl.when(i == 0)
    def _():
        if has_h0:
            carry[...] = h0_ref[0]
        else:
            carry[...] = jnp.zeros_like(carry)

    _rglru_dir(xff, w_ref, br_ref, bi_ref, lam_ref, carry, a_scr, b_scr, hf_ref, 0, False)
    _rglru_dir(xfb, w_ref, br_ref, bi_ref, lam_ref, carry, a_scr, b_scr, hb_ref, 1, True)

    if emit_state:
        @pl.when(i == n_tiles - 1)
        def _():
            sfin_ref[0] = carry[...]


def _rglru(xf, h0, wcat, br, bi, lam, *, bsz, n_tiles, emit_state):
    n_tok = bsz * n_tiles * TILE
    fwd = lambda b, i: (b * n_tiles + i, 0)
    bwd = lambda b, i: (b * n_tiles + n_tiles - 1 - i, 0)
    has_h0 = h0 is not None
    in_specs = [pl.BlockSpec((TILE, D_RG), fwd), pl.BlockSpec((TILE, D_RG), bwd)]
    args = [xf, xf]
    if has_h0:
        in_specs.append(pl.BlockSpec((1, 2, D_RG), lambda b, i: (b, 0, 0)))
        args.append(h0)
    in_specs += [_const_spec((2, N_RG_BLOCKS, RG_BLOCK, 2 * RG_BLOCK)), _const_spec((2, D_RG)),
                 _const_spec((2, D_RG)), _const_spec((2, D_RG))]
    args += [wcat, br, bi, lam]
    out_specs = [pl.BlockSpec((TILE, D_RG), fwd), pl.BlockSpec((TILE, D_RG), bwd)]
    out_shape = [jax.ShapeDtypeStruct((n_tok, D_RG), F32)] * 2
    if emit_state:
        out_specs.append(pl.BlockSpec((1, 2, D_RG), lambda b, i: (b, 0, 0)))
        out_shape.append(jax.ShapeDtypeStruct((bsz, 2, D_RG), F32))
    kern = functools.partial(_rglru_kernel, has_h0=has_h0, emit_state=emit_state, n_tiles=n_tiles)
    return pl.pallas_call(
        kern,
        grid=(bsz, n_tiles),
        in_specs=in_specs,
        out_specs=out_specs,
        out_shape=out_shape,
        scratch_shapes=[pltpu.VMEM((2, D_RG), F32), pltpu.VMEM((TILE, D_RG), F32), pltpu.VMEM((TILE, D_RG), F32)],
        compiler_params=pltpu.CompilerParams(dimension_semantics=("arbitrary", "arbitrary"),
                                             vmem_limit_bytes=VMEM_LIMIT),
        name="rglru_lat" if has_h0 else "rglru_ctx",
    )(*args)


def _out_proj_kernel(x_ref, of_ref, ob_ref, z_ref, hf_ref, hb_ref, gate_ref, mods_ref, dnw_ref, postw_ref,
                     wout_ref, y_ref, mix_scr, tab_scr, *, is_lat, mod_row0):
    b = pl.program_id(0)
    i = pl.program_id(1)
    if is_lat:
        @pl.when((b == 0) & (i == 0))
        def _():
            tab_scr[...] = _pe_table()

    dnw = dnw_ref[...]
    for h in range(N_HEADS):
        cols = slice(h * HEAD_DIM, (h + 1) * HEAD_DIM)
        o = of_ref[h] + ob_ref[h]
        on = o * lax.rsqrt(jnp.mean(o * o, axis=-1, keepdims=True) + EPS) * dnw
        mix_scr[:, cols] = (on * _silu(z_ref[:, cols])).astype(BF16)
    mix_scr[:, D_DN:] = ((hf_ref[...] + hb_ref[...]) * _silu(gate_ref[...])).astype(BF16)

    y = jnp.dot(mix_scr[...], wout_ref[...], preferred_element_type=F32)
    yn = y * lax.rsqrt(jnp.mean(y * y, axis=-1, keepdims=True) + EPS) * postw_ref[...]
    mod = mods_ref[pl.ds(mod_row0 + (b if is_lat else 0), 1), :]
    xs = x_ref[0]
    if is_lat:
        xs = xs + _pe_tile(tab_scr, i)
    y_ref[0] = xs + mod[:, 2 * D_MODEL:] * yn


def _out_proj(x, o_f, o_b, z, h_f, h_b, gate, mods, dnw, postw, wout, *, is_lat, mod_row0):
    bsz, t, _ = x.shape
    n_tiles = t // TILE
    tok = lambda b, i: (b * n_tiles + i, 0)
    head_tok = lambda b, i: (0, b * n_tiles + i, 0)
    kern = functools.partial(_out_proj_kernel, is_lat=is_lat, mod_row0=mod_row0)
    return pl.pallas_call(
        kern,
        grid=(bsz, n_tiles),
        in_specs=[
            pl.BlockSpec((1, TILE, D_MODEL), lambda b, i: (b, i, 0)),
            pl.BlockSpec((N_HEADS, TILE, HEAD_DIM), head_tok),
            pl.BlockSpec((N_HEADS, TILE, HEAD_DIM), head_tok),
            pl.BlockSpec((TILE, D_DN), tok),
            pl.BlockSpec((TILE, D_RG), tok),
            pl.BlockSpec((TILE, D_RG), tok),
            pl.BlockSpec((TILE, D_RG), tok),
            _const_spec((8, 3 * D_MODEL)),
            _const_spec((1, HEAD_DIM)),
            _const_spec((1, D_MODEL)),
            _const_spec((D_DN + D_RG, D_MODEL)),
        ],
        out_specs=pl.BlockSpec((1, TILE, D_MODEL), lambda b, i: (b, i, 0)),
        out_shape=jax.ShapeDtypeStruct((bsz, t, D_MODEL), F32),
        scratch_shapes=[pltpu.VMEM((TILE, D_DN + D_RG), BF16), pltpu.VMEM((GRID_W, D_MODEL // 2), F32)],
        compiler_params=pltpu.CompilerParams(dimension_semantics=("arbitrary", "arbitrary"),
                                             vmem_limit_bytes=VMEM_LIMIT),
        name="out_proj_lat" if is_lat else "out_proj_ctx",
    )(x, o_f, o_b, z, h_f, h_b, gate, mods, dnw, postw, wout)


def _permute_small(w_cols):
    lead = w_cols.shape[:-1]
    ba = w_cols.reshape(lead + (2, 2, N_HEADS))
    per_head = jnp.moveaxis(ba, -1, -3).reshape(lead + (N_HEADS, 4))
    per_head = jnp.pad(per_head, [(0, 0)] * len(lead) + [(0, 0), (0, GB_LANES - 4)])
    flat = per_head.reshape(lead + (N_HEADS * GB_LANES,))
    return jnp.pad(flat, [(0, 0)] * len(lead) + [(0, 128 - N_HEADS * GB_LANES)])


def _mixer_path(x, mods, weights, s_dn0, s_rg0, *, is_lat, mod_row0):
    (prew, postw, win_p, wout, dncw, alog_row, dtb_row, dnw, rgcw, rgcb, wcat, br, bi, lam) = weights
    bsz, t, _ = x.shape
    n_tiles = t // TILE
    q, k, v, z, xf, gate, gb = _in_proj(x, mods, prew, win_p, dncw, rgcw, rgcb, alog_row, dtb_row,
                                        is_lat=is_lat, mod_row0=mod_row0)
    emit = not is_lat
    dn = _delta(q, k, v, gb, s_dn0, bsz=bsz, n_tiles=n_tiles, emit_state=emit)
    rg = _rglru(xf, s_rg0, wcat, br, bi, lam, bsz=bsz, n_tiles=n_tiles, emit_state=emit)
    y = _out_proj(x, dn[0], dn[1], z, rg[0], rg[1], gate, mods, dnw, postw, wout,
                  is_lat=is_lat, mod_row0=mod_row0)
    if emit:
        return y, dn[2], rg[2]
    return y


def kernel(x_prompt, x_sample, state_delta, state_rglru, c, c_ctx, ada_w, ada_b, pre_norm_w, post_norm_w, w_in, w_out, dn_conv_w, dn_a_log, dn_dt_bias, dn_norm_w, rg_conv_w, rg_conv_b, rg_w_r, rg_b_r, rg_w_i, rg_b_i, rg_lam):
    l = 0
    n_lat = c.shape[0]
    c_all = jnp.concatenate([c_ctx[None], c, jnp.zeros((8 - 1 - n_lat, D_MODEL), F32)], axis=0)
    mods = _mods(c_all, ada_w[l].astype(BF16), ada_b[l][None])

    w = w_in[l]
    win_p = jnp.concatenate([w[:, :OFF_B], w[:, OFF_RX:], _permute_small(w[:, OFF_B:OFF_RX])], axis=1).astype(BF16)
    zeros16 = jnp.zeros((2 * N_HEADS,), F32)
    alog_row = _permute_small(jnp.concatenate([zeros16, dn_a_log[l].reshape(-1)]))[None]
    dtb_row = _permute_small(jnp.concatenate([zeros16, dn_dt_bias[l].reshape(-1)]))[None]
    wcat = jnp.concatenate([rg_w_r[l], rg_w_i[l]], axis=-1).astype(BF16)
    weights = (pre_norm_w[l][None], post_norm_w[l][None], win_p, w_out[l].astype(BF16), dn_conv_w[l],
               alog_row, dtb_row, dn_norm_w[l][None], rg_conv_w[l], rg_conv_b[l][None], wcat,
               rg_b_r[l], rg_b_i[l], rg_lam[l])

    y_prompt, s_dn, s_rg = _mixer_path(x_prompt, mods, weights, None, None, is_lat=False, mod_row0=0)
    y_sample = _mixer_path(x_sample, mods, weights, state_delta[:, l], state_rglru[:, l], is_lat=True, mod_row0=1)
    return (y_prompt, y_sample, s_dn[:, None].astype(x_prompt.dtype), s_rg[:, None].astype(x_prompt.dtype))
```

```python
import functools
import math

import jax
import jax.numpy as jnp
from jax import lax
from jax.experimental import pallas as pl
from jax.experimental.pallas import tpu as pltpu

F32 = jnp.float32
BF16 = jnp.bfloat16

D_MODEL = 1024
N_HEADS = 8
HEAD_DIM = 128
D_DN = N_HEADS * HEAD_DIM
D_RG = 1024
N_RG_BLOCKS = 8
RG_BLOCK = D_RG // N_RG_BLOCKS
CONV_W = 4
CONV_PAD_L = 2
CHUNK = 64
RG_C = 8.0
EPS = 1e-6
GRID_W = 64
OFF_Z = 3 * D_DN
OFF_B = 4 * D_DN
OFF_A = OFF_B + 2 * N_HEADS
OFF_RX = OFF_A + 2 * N_HEADS
D_IN = OFF_RX + 2 * D_RG

TILE = 256
HALO = 16
N_CHUNKS = TILE // CHUNK
SEG = 512
GB_LANES = 8
DELTA_HEADS = 8
OUT_ROWS = 512
P_Z = 3 * D_DN
P_RX = P_Z + D_DN
P_RG = P_RX + D_RG
P_SMALL = P_RG + D_RG
D_INP = P_SMALL + 128
NEG_BIG = -1e30
LOG2_E = 1.4426950408889634
LN_2 = 0.6931471805599453
VMEM_LIMIT = 56 * 1024 * 1024


def _sigmoid(x):
    return 0.5 * jnp.tanh(0.5 * x) + 0.5


def _silu(x):
    h = 0.5 * x
    return h * jnp.tanh(h) + h


def _softplus(x):
    return jnp.maximum(x, 0.0) + jnp.log1p(jnp.exp(-jnp.abs(x)))


def _mm(a, b):
    return jnp.dot(a.astype(BF16), b.astype(BF16), preferred_element_type=F32)


def _const_spec(shape):
    nd = len(shape)
    return pl.BlockSpec(shape, lambda *_: (0,) * nd, pipeline_mode=pl.Buffered(1))


def _pe_table():
    quarter = D_MODEL // 4
    j = lax.broadcasted_iota(jnp.int32, (GRID_W, quarter), 0).astype(F32)
    kf = lax.broadcasted_iota(jnp.int32, (GRID_W, quarter), 1).astype(F32)
    freqs = jnp.exp(-math.log(10000.0) * kf / quarter)
    ang = j * freqs
    return jnp.concatenate([jnp.sin(ang), jnp.cos(ang)], axis=1)


def _pe_rows(tab_ref, grid_row, n_rows, col0):
    row_part = jnp.broadcast_to(tab_ref[pl.ds(grid_row, 1), :], (n_rows, D_MODEL // 2))
    col_part = tab_ref[col0:col0 + n_rows, :]
    return jnp.concatenate([row_part, col_part], axis=1)


def _pe_tile(tab_ref, i):
    r0 = i * (TILE // GRID_W)
    return jnp.concatenate([_pe_rows(tab_ref, r0 + j, GRID_W, 0) for j in range(TILE // GRID_W)], axis=0)


def _mods_kernel(c_ref, w_ref, b_ref, o_ref):
    o_ref[...] = _mm(_silu(c_ref[...]), w_ref[...]) + b_ref[...]


def _mods(c_all, ada_w, ada_b):
    return pl.pallas_call(
        _mods_kernel,
        grid=(3,),
        in_specs=[
            pl.BlockSpec((8, D_MODEL), lambda j: (0, 0)),
            pl.BlockSpec((D_MODEL, D_MODEL), lambda j: (0, j)),
            pl.BlockSpec((1, D_MODEL), lambda j: (0, j)),
        ],
        out_specs=pl.BlockSpec((8, D_MODEL), lambda j: (0, j)),
        out_shape=jax.ShapeDtypeStruct((8, 3 * D_MODEL), F32),
        compiler_params=pltpu.CompilerParams(dimension_semantics=("arbitrary",), vmem_limit_bytes=VMEM_LIMIT),
        name="mods",
    )(c_all, ada_w, ada_b)


def _in_proj_kernel(xp_ref, x_ref, xn_ref, mods_ref, prew_ref, win_ref, dncw_ref, rgcw_ref, rgcb_ref,
                    alog_ref, dtb_ref,
                    q_ref, k_ref, v_ref, z_ref, xf_ref, gate_ref, gb_ref,
                    h_scr, tri_scr, tab_scr, *, is_lat, mod_row0, n_tiles):
    b = pl.program_id(0)
    i = pl.program_id(1)

    @pl.when((b == 0) & (i == 0))
    def _():
        r = lax.broadcasted_iota(jnp.int32, (TILE, TILE), 0)
        c = lax.broadcasted_iota(jnp.int32, (TILE, TILE), 1)
        tri_scr[...] = jnp.where(((r // CHUNK) == (c // CHUNK)) & (r >= c), 1.0, 0.0).astype(BF16)
        if is_lat:
            tab_scr[...] = _pe_table()

    mod = mods_ref[pl.ds(mod_row0 + (b if is_lat else 0), 1), :]
    shift = mod[:, :D_MODEL]
    scale = mod[:, D_MODEL:2 * D_MODEL]
    prew = prew_ref[...]

    def norm_mod(xv):
        ms = jnp.mean(xv * xv, axis=-1, keepdims=True)
        return (xv * lax.rsqrt(ms + EPS) * prew) * (1.0 + scale) + shift

    x_prev = xp_ref[0]
    x_main = x_ref[0]
    x_next = xn_ref[0]
    if is_lat:
        r0 = i * (TILE // GRID_W)
        x_main = x_main + _pe_tile(tab_scr, i)
        x_prev = x_prev + _pe_rows(tab_scr, jnp.maximum(r0 - 1, 0), HALO, GRID_W - HALO)
        x_next = x_next + _pe_rows(tab_scr, jnp.minimum(r0 + TILE // GRID_W, GRID_W - 1), HALO, 0)

    h_scr[0:HALO, :] = jnp.where(i > 0, norm_mod(x_prev), 0.0).astype(BF16)
    h_scr[HALO:HALO + TILE, :] = norm_mod(x_main).astype(BF16)
    h_scr[HALO + TILE:, :] = jnp.where(i < n_tiles - 1, norm_mod(x_next), 0.0).astype(BF16)

    def conv_seg(col0, cw_ref, cw_col0):
        p = jnp.dot(h_scr[...], win_ref[:, col0:col0 + SEG], preferred_element_type=F32)
        acc = None
        for j in range(CONV_W):
            shifted = p if j == CONV_PAD_L else pltpu.roll(p, (CONV_PAD_L - j) % (TILE + 2 * HALO), 0)
            tap = shifted[HALO:HALO + TILE, :] * cw_ref[j:j + 1, cw_col0:cw_col0 + SEG]
            acc = tap if acc is None else acc + tap
        return acc

    def plain_seg(col0, width):
        return jnp.dot(h_scr[HALO:HALO + TILE, :], win_ref[:, col0:col0 + width], preferred_element_type=F32)

    raw = plain_seg(P_SMALL, 128)
    beta = _sigmoid(raw)
    g = -jnp.exp(alog_ref[...]) * _softplus(raw + dtb_ref[...])
    g_hi = g.astype(BF16)
    g_r1 = g - g_hi.astype(F32)
    g_mid = g_r1.astype(BF16)
    g_lo = (g_r1 - g_mid.astype(F32)).astype(BF16)

    def qkv_seg(s):
        act = _silu(conv_seg(s * SEG, dncw_ref, s * SEG))
        which = (s * SEG) // D_DN
        out_ref = (q_ref, k_ref, v_ref)[which]
        for hl in range(SEG // HEAD_DIM):
            head = ((s * SEG) % D_DN) // HEAD_DIM + hl
            xh = act[:, hl * HEAD_DIM:(hl + 1) * HEAD_DIM]
            if which < 2:
                xh = xh * lax.rsqrt(jnp.sum(xh * xh, axis=-1, keepdims=True) + EPS)
            out_ref[head] = xh

    qkv_seg(0)

    lower = tri_scr[...]
    cum_f = (jnp.dot(lower, g_hi, preferred_element_type=F32) + jnp.dot(lower, g_mid, preferred_element_type=F32)
             + jnp.dot(lower, g_lo, preferred_element_type=F32))
    tot = jnp.concatenate([jnp.broadcast_to(cum_f[ci * CHUNK + CHUNK - 1:(ci + 1) * CHUNK, :], (CHUNK, 128))
                           for ci in range(N_CHUNKS)], axis=0)
    cum_b = tot - cum_f + g
    lane = lax.broadcasted_iota(jnp.int32, (TILE, 128), 1) % GB_LANES
    comb = jnp.where(lane < 2, beta, jnp.where(lane == 2, cum_f, jnp.where(lane == 3, cum_b, 0.0)))
    for h in range(N_HEADS):
        gb_ref[h] = comb[:, h * GB_LANES:(h + 1) * GB_LANES]

    for s in range(1, 3 * D_DN // SEG):
        qkv_seg(s)
    for s in range(D_RG // SEG):
        xf_ref[:, s * SEG:(s + 1) * SEG] = (conv_seg(P_RX + s * SEG, rgcw_ref, s * SEG)
                                            + rgcb_ref[:, s * SEG:(s + 1) * SEG])
    for s in range(D_DN // SEG):
        z_ref[:, s * SEG:(s + 1) * SEG] = plain_seg(P_Z + s * SEG, SEG)
    for s in range(D_RG // SEG):
        gate_ref[:, s * SEG:(s + 1) * SEG] = plain_seg(P_RG + s * SEG, SEG)


def _in_proj(x, mods, prew, win_p, dncw, rgcw, rgcb, alog_row, dtb_row, *, is_lat, mod_row0):
    bsz, t, _ = x.shape
    n_tiles = t // TILE
    n_tok = bsz * t
    hb = TILE // HALO
    tok = lambda b, i: (b * n_tiles + i, 0)
    head_tok = lambda b, i: (0, b * n_tiles + i, 0)
    kern = functools.partial(_in_proj_kernel, is_lat=is_lat, mod_row0=mod_row0, n_tiles=n_tiles)
    return pl.pallas_call(
        kern,
        grid=(bsz, n_tiles),
        in_specs=[
            pl.BlockSpec((1, HALO, D_MODEL), lambda b, i: (b, jnp.maximum(i * hb - 1, 0), 0)),
            pl.BlockSpec((1, TILE, D_MODEL), lambda b, i: (b, i, 0)),
            pl.BlockSpec((1, HALO, D_MODEL), lambda b, i: (b, jnp.minimum((i + 1) * hb, t // HALO - 1), 0)),
            _const_spec((8, 3 * D_MODEL)),
            _const_spec((1, D_MODEL)),
            _const_spec((D_MODEL, D_INP)),
            _const_spec((CONV_W, 3 * D_DN)),
            _const_spec((CONV_W, D_RG)),
            _const_spec((1, D_RG)),
            _const_spec((1, 128)),
            _const_spec((1, 128)),
        ],
        out_specs=[
            pl.BlockSpec((N_HEADS, TILE, HEAD_DIM), head_tok),
            pl.BlockSpec((N_HEADS, TILE, HEAD_DIM), head_tok),
            pl.BlockSpec((N_HEADS, TILE, HEAD_DIM), head_tok),
            pl.BlockSpec((TILE, D_DN), tok),
            pl.BlockSpec((TILE, D_RG), tok),
            pl.BlockSpec((TILE, D_RG), tok),
            pl.BlockSpec((N_HEADS, TILE, GB_LANES), head_tok),
        ],
        out_shape=[
            jax.ShapeDtypeStruct((N_HEADS, n_tok, HEAD_DIM), F32),
            jax.ShapeDtypeStruct((N_HEADS, n_tok, HEAD_DIM), F32),
            jax.ShapeDtypeStruct((N_HEADS, n_tok, HEAD_DIM), F32),
            jax.ShapeDtypeStruct((n_tok, D_DN), F32),
            jax.ShapeDtypeStruct((n_tok, D_RG), F32),
            jax.ShapeDtypeStruct((n_tok, D_RG), F32),
            jax.ShapeDtypeStruct((N_HEADS, n_tok, GB_LANES), F32),
        ],
        scratch_shapes=[
            pltpu.VMEM((TILE + 2 * HALO, D_MODEL), BF16),
            pltpu.VMEM((TILE, TILE), BF16),
            pltpu.VMEM((GRID_W, D_MODEL // 2), F32),
        ],
        compiler_params=pltpu.CompilerParams(dimension_semantics=("arbitrary", "arbitrary"),
                                             vmem_limit_bytes=VMEM_LIMIT),
        name="in_proj_lat" if is_lat else "in_proj_ctx",
    )(x, x, x, mods, prew, win_p, dncw, rgcw, rgcb, alog_row, dtb_row)


K_EYE, K_INCL_F, K_INCL_B, K_STRICT_F, K_STRICT_B, K_LEVEL0 = 0, 1, 2, 3, 4, 5
N_LEVELS = int(math.log2(CHUNK))
N_CAT_CONSTS = K_LEVEL0 + N_LEVELS


def _delta_consts(cst, half):
    r = lax.broadcasted_iota(jnp.int32, (CHUNK, TILE), 0)
    c = lax.broadcasted_iota(jnp.int32, (CHUNK, TILE), 1) % CHUNK
    cst[K_EYE] = jnp.where(r == c, 1.0, 0.0)
    cst[K_INCL_F] = jnp.where(r >= c, 0.0, NEG_BIG)
    cst[K_INCL_B] = jnp.where(r <= c, 0.0, NEG_BIG)
    cst[K_STRICT_F] = jnp.where(r > c, 0.0, NEG_BIG)
    cst[K_STRICT_B] = jnp.where(r < c, 0.0, NEG_BIG)
    for lv in range(N_LEVELS):
        m = 1 << lv
        cst[K_LEVEL0 + lv] = jnp.where(((r // (2 * m)) == (c // (2 * m))) & ((r // m) != (c // m)), 1.0, 0.0)
    lane = lax.broadcasted_iota(jnp.int32, (CHUNK, 128), 1)
    half[0] = jnp.where(lane < CHUNK, 1.0, 0.0).astype(BF16)
    half[1] = jnp.where(lane >= CHUNK, 1.0, 0.0).astype(BF16)


def _cat_of_diag_blocks(full):
    lo = lax.broadcasted_iota(jnp.int32, (CHUNK, 128), 1) < CHUNK
    tiles = []
    for t in range(TILE // 128):
        a = full[(2 * t) * CHUNK:(2 * t + 1) * CHUNK, t * 128:(t + 1) * 128]
        b = full[(2 * t + 1) * CHUNK:(2 * t + 2) * CHUNK, t * 128:(t + 1) * 128]
        tiles.append(jnp.where(lo, a, b))
    return jnp.concatenate(tiles, axis=1)


def _cat_of_columns(col):
    lo = lax.broadcasted_iota(jnp.int32, (CHUNK, 128), 1) < CHUNK
    tiles = []
    for t in range(TILE // 128):
        a = jnp.broadcast_to(col[(2 * t) * CHUNK:(2 * t + 1) * CHUNK, :], (CHUNK, 128))
        b = jnp.broadcast_to(col[(2 * t + 1) * CHUNK:(2 * t + 2) * CHUNK, :], (CHUNK, 128))
        tiles.append(jnp.where(lo, a, b))
    return jnp.concatenate(tiles, axis=1)


def _block_diag(cat, half):
    zero = jnp.zeros((CHUNK, 128), BF16)
    rows = []
    for j in range(N_CHUNKS):
        t = j // 2
        blk = cat[:, t * 128:(t + 1) * 128] * half[j % 2]
        rows.append(jnp.concatenate([blk if tt == t else zero for tt in range(TILE // 128)], axis=1))
    return jnp.concatenate(rows, axis=0)


def _delta_tile(probs, s_ref, cst, half):
    n = len(probs)
    eye8 = jnp.where(lax.broadcasted_iota(jnp.int32, (GB_LANES, GB_LANES), 0)
                     == lax.broadcasted_iota(jnp.int32, (GB_LANES, GB_LANES), 1), 1.0, 0.0)
    beta, gcum, qs, kbeta, attn, ncat = [], [], [], [], [], []
    for q, k, v, gb, hl, d in probs:
        beta.append(gb[:, d:d + 1])
        gcum.append(gb[:, 2 + d:3 + d])
        gb_t = lax.dot_general(eye8, gb, (((1,), (1,)), ((), ())), precision=lax.Precision.HIGHEST,
                               preferred_element_type=F32)
        diff = _cat_of_columns(gcum[-1]) - gb_t[2 + d:3 + d, :]
        qs.append(q * (HEAD_DIM ** -0.5))
        kbeta.append(k * beta[-1])
        prod = lax.dot_general(jnp.concatenate([qs[-1], kbeta[-1]], axis=0).astype(BF16), k.astype(BF16),
                               (((1,), (1,)), ((), ())), preferred_element_type=F32)
        attn.append(_cat_of_diag_blocks(prod[:TILE]) * jnp.exp(diff + cst[K_INCL_B if d else K_INCL_F]))
        ncat.append(_cat_of_diag_blocks(prod[TILE:]) * jnp.exp(diff + cst[K_STRICT_B if d else K_STRICT_F]))

    xcat = [cst[K_EYE] - ncat[p] * cst[K_LEVEL0] for p in range(n)]
    xbd = [_block_diag(xcat[p].astype(BF16), half) for p in range(n)]
    for lv in range(1, N_LEVELS):
        cbd = [_block_diag((ncat[p] * cst[K_LEVEL0 + lv]).astype(BF16), half) for p in range(n)]
        xc = [jnp.dot(xcat[p].astype(BF16), cbd[p], preferred_element_type=F32) for p in range(n)]
        xcat = [xcat[p] - jnp.dot(xc[p].astype(BF16), xbd[p], preferred_element_type=F32) for p in range(n)]
        xbd = [_block_diag(xcat[p].astype(BF16), half) for p in range(n)]

    u, w, qd, kd, tot = [], [], [], [], []
    for p, (q, k, v, gb, hl, d) in enumerate(probs):
        eg = jnp.exp(gcum[p])
        rhs = jnp.concatenate([v * beta[p], kbeta[p] * eg], axis=1).astype(BF16)
        sol = jnp.dot(xbd[p], rhs, preferred_element_type=F32)
        u.append(sol[:, :HEAD_DIM])
        w.append(sol[:, HEAD_DIM:])
        qd.append(qs[p] * eg)
        last = [ci * CHUNK if d else ci * CHUNK + CHUNK - 1 for ci in range(N_CHUNKS)]
        tot.append([gcum[p][r0:r0 + 1, :] for r0 in last])
        gtot = jnp.concatenate([jnp.broadcast_to(x, (CHUNK, 1)) for x in tot[p]], axis=0)
        kd.append(k * jnp.exp(gtot - gcum[p]))

    s = [s_ref[d, hl] for (_, _, _, _, hl, d) in probs]
    vnew = [[None] * N_CHUNKS for _ in range(n)]
    obase = [[None] * N_CHUNKS for _ in range(n)]
    for step in range(N_CHUNKS):
        for p, (_, _, _, _, hl, d) in enumerate(probs):
            ci = N_CHUNKS - 1 - step if d else step
            rows = slice(ci * CHUNK, (ci + 1) * CHUNK)
            ws = _mm(jnp.concatenate([w[p][rows], qd[p][rows]], axis=0), s[p])
            vn = u[p][rows] - ws[:CHUNK]
            obase[p][ci] = ws[CHUNK:]
            vnew[p][ci] = vn.astype(BF16)
            upd = lax.dot_general(kd[p][rows].astype(BF16), vnew[p][ci], (((0,), (0,)), ((), ())),
                                  preferred_element_type=F32)
            s[p] = s[p] * jnp.exp(tot[p][ci]) + upd
    outs = []
    zero = jnp.zeros((CHUNK, HEAD_DIM), BF16)
    for p, (_, _, _, _, hl, d) in enumerate(probs):
        s_ref[d, hl] = s[p]
        vbd = jnp.concatenate([jnp.concatenate([vnew[p][j] if jj == j else zero for jj in range(N_CHUNKS)], axis=1)
                               for j in range(N_CHUNKS)], axis=0)
        ocat = jnp.dot(attn[p].astype(BF16), vbd, preferred_element_type=F32)
        outs.append(jnp.concatenate([obase[p][j] + ocat[:, j * HEAD_DIM:(j + 1) * HEAD_DIM]
                                     for j in range(N_CHUNKS)], axis=0))
    return outs


def _delta_kernel(*refs, has_s0, emit_state, n_tiles):
    if has_s0:
        (qf, kf, vf, gbf, qb, kb, vb, gbb, s0_ref), rest = refs[:9], refs[9:]
    else:
        (qf, kf, vf, gbf, qb, kb, vb, gbb), rest = refs[:8], refs[8:]
        s0_ref = None
    if emit_state:
        of_ref, ob_ref, sout_ref, s_scr, cst, half = rest
    else:
        of_ref, ob_ref, s_scr, cst, half = rest
        sout_ref = None
    i = pl.program_id(2)

    @pl.when((pl.program_id(0) == 0) & (pl.program_id(1) == 0) & (i == 0))
    def _():
        _delta_consts(cst, half)

    @pl.when(i == 0)
    def _():
        if has_s0:
            s_scr[...] = s0_ref[0]
        else:
            s_scr[...] = jnp.zeros_like(s_scr)

    probs = []
    for hl in range(DELTA_HEADS):
        probs.append((qf[hl], kf[hl], vf[hl], gbf[hl], hl, 0))
        probs.append((qb[hl], kb[hl], vb[hl], gbb[hl], hl, 1))
    outs = _delta_tile(probs, s_scr, cst, half)
    for hl in range(DELTA_HEADS):
        of_ref[hl] = outs[2 * hl]
        ob_ref[hl] = outs[2 * hl + 1]

    if emit_state:
        @pl.when(i == n_tiles - 1)
        def _():
            sout_ref[0] = s_scr[...]


def _delta(q, k, v, gb, s0, *, bsz, n_tiles, emit_state):
    n_tok = bsz * n_tiles * TILE
    hb = DELTA_HEADS
    fwd = lambda b, h, i: (h, b * n_tiles + i, 0)
    bwd = lambda b, h, i: (h, b * n_tiles + n_tiles - 1 - i, 0)
    has_s0 = s0 is not None
    in_specs, args = [], []
    for m in (fwd, bwd):
        in_specs += [pl.BlockSpec((hb, TILE, HEAD_DIM), m)] * 3 + [pl.BlockSpec((hb, TILE, GB_LANES), m)]
        args += [q, k, v, gb]
    state_spec = pl.BlockSpec((1, 2, hb, HEAD_DIM, HEAD_DIM), lambda b, h, i: (b, 0, h, 0, 0))
    if has_s0:
        in_specs.append(state_spec)
        args.append(s0)
    out_specs = [pl.BlockSpec((hb, TILE, HEAD_DIM), fwd), pl.BlockSpec((hb, TILE, HEAD_DIM), bwd)]
    out_shape = [jax.ShapeDtypeStruct((N_HEADS, n_tok, HEAD_DIM), F32)] * 2
    if emit_state:
        out_specs.append(state_spec)
        out_shape.append(jax.ShapeDtypeStruct((bsz, 2, N_HEADS, HEAD_DIM, HEAD_DIM), F32))
    kern = functools.partial(_delta_kernel, has_s0=has_s0, emit_state=emit_state, n_tiles=n_tiles)
    return pl.pallas_call(
        kern,
        grid=(bsz, N_HEADS // hb, n_tiles),
        in_specs=in_specs,
        out_specs=out_specs,
        out_shape=out_shape,
        scratch_shapes=[pltpu.VMEM((2, hb, HEAD_DIM, HEAD_DIM), F32), pltpu.VMEM((N_CAT_CONSTS, CHUNK, TILE), F32),
                        pltpu.VMEM((2, CHUNK, 128), BF16)],
        compiler_params=pltpu.CompilerParams(dimension_semantics=("arbitrary", "arbitrary", "arbitrary"),
                                             vmem_limit_bytes=VMEM_LIMIT),
        name="delta_lat" if has_s0 else "delta_ctx",
    )(*args)


RG_SUB = D_RG // 128


def _rglru_gates(xf_ref, w_ref, br_ref, bi_ref, lam_ref, a_scr, b_scr, d):
    for n in range(N_RG_BLOCKS):
        cols = slice(n * RG_BLOCK, (n + 1) * RG_BLOCK)
        xb = xf_ref[:, cols]
        ri = jnp.dot(xb.astype(BF16), w_ref[d, n], preferred_element_type=F32)
        tr = jnp.tanh(0.5 * ri[:, :RG_BLOCK] + 0.5 * br_ref[d:d + 1, cols])
        ti = jnp.tanh(0.5 * ri[:, RG_BLOCK:] + 0.5 * bi_ref[d:d + 1, cols])
        c2 = (-0.5 * RG_C * LOG2_E) * _softplus(-lam_ref[d:d + 1, cols])
        log2_a = c2 * tr + c2
        a = jnp.exp2(log2_a)
        one_minus_a2 = jnp.tanh((-LN_2) * log2_a) * (a * a + 1.0)
        gain = jnp.where(one_minus_a2 > 0.0, one_minus_a2 * lax.rsqrt(one_minus_a2), 0.0)
        half_x = 0.5 * xb
        a_scr[pl.ds(n, TILE, stride=RG_SUB), :] = a
        b_scr[pl.ds(n, TILE, stride=RG_SUB), :] = gain * (half_x * ti + half_x)


def _rglru_kernel(*refs, has_h0, emit_state, n_tiles):
    if has_h0:
        (xff, xfb, h0_ref), rest = refs[:3], refs[3:]
    else:
        (xff, xfb), rest = refs[:2], refs[2:]
        h0_ref = None
    w_ref, br_ref, bi_ref, lam_ref = rest[:4]
    rest = rest[4:]
    if emit_state:
        hf_ref, hb_ref, sfin_ref, carry, af_scr, bf_scr, ab_scr, bb_scr = rest
    else:
        hf_ref, hb_ref, carry, af_scr, bf_scr, ab_scr, bb_scr = rest
        sfin_ref = None
    i = pl.program_id(1)

    @pl.when(i == 0)
    def _():
        if has_h0:
            carry[...] = h0_ref[0]
        else:
            carry[...] = jnp.zeros_like(carry)

    _rglru_gates(xff, w_ref, br_ref, bi_ref, lam_ref, af_scr, bf_scr, 0)
    _rglru_gates(xfb, w_ref, br_ref, bi_ref, lam_ref, ab_scr, bb_scr, 1)

    def body(g, hs):
        hf, hb = hs
        for r in range(8):
            tf = g * 8 + r
            rows = pl.ds(pl.multiple_of(tf * RG_SUB, RG_SUB), RG_SUB)
            hf = af_scr[rows, :] * hf + bf_scr[rows, :]
            hf_ref[tf] = hf
            tb = TILE - 1 - tf
            rows = pl.ds(pl.multiple_of(tb * RG_SUB, RG_SUB), RG_SUB)
            hb = ab_scr[rows, :] * hb + bb_scr[rows, :]
            hb_ref[tb] = hb
        return hf, hb

    hf, hb = lax.fori_loop(0, TILE // 8, body, (carry[0], carry[1]))
    carry[0] = hf
    carry[1] = hb

    if emit_state:
        @pl.when(i == n_tiles - 1)
        def _():
            sfin_ref[0] = carry[...]


def _rglru(xf, h0, wcat, br, bi, lam, *, bsz, n_tiles, emit_state):
    n_tok = bsz * n_tiles * TILE
    fwd = lambda b, i: (b * n_tiles + i, 0)
    bwd = lambda b, i: (b * n_tiles + n_tiles - 1 - i, 0)
    fwd3 = lambda b, i: (b * n_tiles + i, 0, 0)
    bwd3 = lambda b, i: (b * n_tiles + n_tiles - 1 - i, 0, 0)
    state_spec = pl.BlockSpec((1, 2, RG_SUB, 128), lambda b, i: (b, 0, 0, 0))
    has_h0 = h0 is not None
    in_specs = [pl.BlockSpec((TILE, D_RG), fwd), pl.BlockSpec((TILE, D_RG), bwd)]
    args = [xf, xf]
    if has_h0:
        in_specs.append(state_spec)
        args.append(h0.reshape(bsz, 2, RG_SUB, 128))
    in_specs += [_const_spec((2, N_RG_BLOCKS, RG_BLOCK, 2 * RG_BLOCK)), _const_spec((2, D_RG)),
                 _const_spec((2, D_RG)), _const_spec((2, D_RG))]
    args += [wcat, br, bi, lam]
    out_specs = [pl.BlockSpec((TILE, RG_SUB, 128), fwd3), pl.BlockSpec((TILE, RG_SUB, 128), bwd3)]
    out_shape = [jax.ShapeDtypeStruct((n_tok, RG_SUB, 128), F32)] * 2
    if emit_state:
        out_specs.append(state_spec)
        out_shape.append(jax.ShapeDtypeStruct((bsz, 2, RG_SUB, 128), F32))
    kern = functools.partial(_rglru_kernel, has_h0=has_h0, emit_state=emit_state, n_tiles=n_tiles)
    return pl.pallas_call(
        kern,
        grid=(bsz, n_tiles),
        in_specs=in_specs,
        out_specs=out_specs,
        out_shape=out_shape,
        scratch_shapes=[pltpu.VMEM((2, RG_SUB, 128), F32)] + [pltpu.VMEM((TILE * RG_SUB, 128), F32)] * 4,
        compiler_params=pltpu.CompilerParams(dimension_semantics=("arbitrary", "arbitrary"),
                                             vmem_limit_bytes=VMEM_LIMIT),
        name="rglru_lat" if has_h0 else "rglru_ctx",
    )(*args)


def _out_proj_kernel(x_ref, of_ref, ob_ref, z_ref, hf_ref, hb_ref, gate_ref, mods_ref, dnw_ref, postw_ref,
                     wout_ref, y_ref, mix_scr, tab_scr, *, is_lat, mod_row0, blocks_per_seq):
    i = pl.program_id(0)
    if is_lat:
        @pl.when(i == 0)
        def _():
            tab_scr[...] = _pe_table()

    dnw = dnw_ref[...]
    for h in range(N_HEADS):
        cols = slice(h * HEAD_DIM, (h + 1) * HEAD_DIM)
        o = of_ref[h] + ob_ref[h]
        on = o * lax.rsqrt(jnp.mean(o * o, axis=-1, keepdims=True) + EPS) * dnw
        mix_scr[:, cols] = (on * _silu(z_ref[:, cols])).astype(BF16)
    mix_scr[:, D_DN:] = ((hf_ref[...] + hb_ref[...]) * _silu(gate_ref[...])).astype(BF16)

    y = jnp.dot(mix_scr[...], wout_ref[...], preferred_element_type=F32)
    yn = y * lax.rsqrt(jnp.mean(y * y, axis=-1, keepdims=True) + EPS) * postw_ref[...]
    mod = mods_ref[pl.ds(mod_row0 + (i // blocks_per_seq if is_lat else 0), 1), :]
    xs = x_ref[...]
    if is_lat:
        first_tile = (i % blocks_per_seq) * (OUT_ROWS // TILE)
        xs = xs + jnp.concatenate([_pe_tile(tab_scr, first_tile + j) for j in range(OUT_ROWS // TILE)], axis=0)
    y_ref[...] = xs + mod[:, 2 * D_MODEL:] * yn


def _out_proj(x, o_f, o_b, z, h_f, h_b, gate, mods, dnw, postw, wout, *, is_lat, mod_row0):
    bsz, t, _ = x.shape
    n_tok = bsz * t
    assert n_tok % OUT_ROWS == 0 and (t % OUT_ROWS == 0 or not is_lat)
    tok = lambda i: (i, 0)
    head_tok = lambda i: (0, i, 0)
    kern = functools.partial(_out_proj_kernel, is_lat=is_lat, mod_row0=mod_row0, blocks_per_seq=max(t // OUT_ROWS, 1))
    y = pl.pallas_call(
        kern,
        grid=(n_tok // OUT_ROWS,),
        in_specs=[
            pl.BlockSpec((OUT_ROWS, D_MODEL), tok),
            pl.BlockSpec((N_HEADS, OUT_ROWS, HEAD_DIM), head_tok),
            pl.BlockSpec((N_HEADS, OUT_ROWS, HEAD_DIM), head_tok),
            pl.BlockSpec((OUT_ROWS, D_DN), tok),
            pl.BlockSpec((OUT_ROWS, D_RG), tok),
            pl.BlockSpec((OUT_ROWS, D_RG), tok),
            pl.BlockSpec((OUT_ROWS, D_RG), tok),
            _const_spec((8, 3 * D_MODEL)),
            _const_spec((1, HEAD_DIM)),
            _const_spec((1, D_MODEL)),
            _const_spec((D_DN + D_RG, D_MODEL)),
        ],
        out_specs=pl.BlockSpec((OUT_ROWS, D_MODEL), tok),
        out_shape=jax.ShapeDtypeStruct((n_tok, D_MODEL), F32),
        scratch_shapes=[pltpu.VMEM((OUT_ROWS, D_DN + D_RG), BF16), pltpu.VMEM((GRID_W, D_MODEL // 2), F32)],
        compiler_params=pltpu.CompilerParams(dimension_semantics=("arbitrary",), vmem_limit_bytes=VMEM_LIMIT),
        name="out_proj_lat" if is_lat else "out_proj_ctx",
    )(x.reshape(n_tok, D_MODEL), o_f, o_b, z, h_f, h_b, gate, mods, dnw, postw, wout)
    return y.reshape(bsz, t, D_MODEL)


def _permute_small(w_cols):
    lead = w_cols.shape[:-1]
    ba = w_cols.reshape(lead + (2, 2, N_HEADS))
    per_head = jnp.moveaxis(ba, -1, -3).reshape(lead + (N_HEADS, 4))
    per_head = jnp.pad(per_head, [(0, 0)] * len(lead) + [(0, 0), (0, GB_LANES - 4)])
    flat = per_head.reshape(lead + (N_HEADS * GB_LANES,))
    return jnp.pad(flat, [(0, 0)] * len(lead) + [(0, 128 - N_HEADS * GB_LANES)])


def _mixer_path(x, mods, weights, s_dn0, s_rg0, *, is_lat, mod_row0):
    (prew, postw, win_p, wout, dncw, alog_row, dtb_row, dnw, rgcw, rgcb, wcat, br, bi, lam) = weights
    bsz, t, _ = x.shape
    n_tiles = t // TILE
    q, k, v, z, xf, gate, gb = _in_proj(x, mods, prew, win_p, dncw, rgcw, rgcb, alog_row, dtb_row,
                                        is_lat=is_lat, mod_row0=mod_row0)
    emit = not is_lat
    dn = _delta(q, k, v, gb, s_dn0, bsz=bsz, n_tiles=n_tiles, emit_state=emit)
    rg = _rglru(xf, s_rg0, wcat, br, bi, lam, bsz=bsz, n_tiles=n_tiles, emit_state=emit)
    y = _out_proj(x, dn[0], dn[1], z, rg[0].reshape(-1, D_RG), rg[1].reshape(-1, D_RG), gate, mods, dnw, postw, wout,
                  is_lat=is_lat, mod_row0=mod_row0)
    if emit:
        return y, dn[2], rg[2].reshape(bsz, 2, D_RG)
    return y


def kernel(x_prompt, x_sample, state_delta, state_rglru, c, c_ctx, ada_w, ada_b, pre_norm_w, post_norm_w, w_in, w_out, dn_conv_w, dn_a_log, dn_dt_bias, dn_norm_w, rg_conv_w, rg_conv_b, rg_w_r, rg_b_r, rg_w_i, rg_b_i, rg_lam):
    l = 0
    n_lat = c.shape[0]
    c_all = jnp.concatenate([c_ctx[None], c, jnp.zeros((8 - 1 - n_lat, D_MODEL), F32)], axis=0)
    mods = _mods(c_all, ada_w[l].astype(BF16), ada_b[l][None])

    w = w_in[l]
    win_p = jnp.concatenate([w[:, :OFF_B], w[:, OFF_RX:], _permute_small(w[:, OFF_B:OFF_RX])], axis=1).astype(BF16)
    zeros16 = jnp.zeros((2 * N_HEADS,), F32)
    alog_row = _permute_small(jnp.concatenate([zeros16, dn_a_log[l].reshape(-1)]))[None]
    dtb_row = _permute_small(jnp.concatenate([zeros16, dn_dt_bias[l].reshape(-1)]))[None]
    wcat = jnp.concatenate([rg_w_r[l], rg_w_i[l]], axis=-1).astype(BF16)
    weights = (pre_norm_w[l][None], post_norm_w[l][None], win_p, w_out[l].astype(BF16), dn_conv_w[l],
               alog_row, dtb_row, dn_norm_w[l][None], rg_conv_w[l], rg_conv_b[l][None], wcat,
               rg_b_r[l], rg_b_i[l], rg_lam[l])

    y_prompt, s_dn, s_rg = _mixer_path(x_prompt, mods, weights, None, None, is_lat=False, mod_row0=0)
    y_sample = _mixer_path(x_sample, mods, weights, state_delta[:, l], state_rglru[:, l], is_lat=True, mod_row0=1)
    return (y_prompt, y_sample, s_dn[:, None].astype(x_prompt.dtype), s_rg[:, None].astype(x_prompt.dtype))
```

```python
import functools
import math

import jax
import jax.numpy as jnp
from jax import lax
from jax.experimental import pallas as pl
from jax.experimental.pallas import tpu as pltpu

F32 = jnp.float32
BF16 = jnp.bfloat16

D_MODEL = 1024
N_HEADS = 8
HEAD_DIM = 128
D_DN = N_HEADS * HEAD_DIM
D_RG = 1024
N_RG_BLOCKS = 8
RG_BLOCK = D_RG // N_RG_BLOCKS
CONV_W = 4
CONV_PAD_L = 2
CHUNK = 64
RG_C = 8.0
EPS = 1e-6
GRID_W = 64
OFF_Z = 3 * D_DN
OFF_B = 4 * D_DN
OFF_A = OFF_B + 2 * N_HEADS
OFF_RX = OFF_A + 2 * N_HEADS
D_IN = OFF_RX + 2 * D_RG

TILE = 256
HALO = 16
N_CHUNKS = TILE // CHUNK
SEG = 512
GB_LANES = 8
DELTA_HEADS = 8
OUT_ROWS = 512
P_Z = 3 * D_DN
P_RX = P_Z + D_DN
P_RG = P_RX + D_RG
P_SMALL = P_RG + D_RG
D_INP = P_SMALL + 128
NEG_BIG = -1e30
LOG2_E = 1.4426950408889634
LN_2 = 0.6931471805599453
VMEM_LIMIT = 56 * 1024 * 1024


def _sigmoid(x):
    return 0.5 * jnp.tanh(0.5 * x) + 0.5


def _silu(x):
    h = 0.5 * x
    return h * jnp.tanh(h) + h


def _softplus(x):
    return jnp.maximum(x, 0.0) + jnp.log1p(jnp.exp(-jnp.abs(x)))


def _mm(a, b):
    return jnp.dot(a.astype(BF16), b.astype(BF16), preferred_element_type=F32)


def _const_spec(shape):
    nd = len(shape)
    return pl.BlockSpec(shape, lambda *_: (0,) * nd, pipeline_mode=pl.Buffered(1))


def _pe_table():
    quarter = D_MODEL // 4
    j = lax.broadcasted_iota(jnp.int32, (GRID_W, quarter), 0).astype(F32)
    kf = lax.broadcasted_iota(jnp.int32, (GRID_W, quarter), 1).astype(F32)
    freqs = jnp.exp(-math.log(10000.0) * kf / quarter)
    ang = j * freqs
    return jnp.concatenate([jnp.sin(ang), jnp.cos(ang)], axis=1)


def _pe_rows(tab_ref, grid_row, n_rows, col0):
    row_part = jnp.broadcast_to(tab_ref[pl.ds(grid_row, 1), :], (n_rows, D_MODEL // 2))
    col_part = tab_ref[col0:col0 + n_rows, :]
    return jnp.concatenate([row_part, col_part], axis=1)


def _pe_tile(tab_ref, i):
    r0 = i * (TILE // GRID_W)
    return jnp.concatenate([_pe_rows(tab_ref, r0 + j, GRID_W, 0) for j in range(TILE // GRID_W)], axis=0)


def _mods_kernel(c_ref, w_ref, b_ref, o_ref):
    o_ref[...] = _mm(_silu(c_ref[...]), w_ref[...]) + b_ref[...]


def _mods(c_all, ada_w, ada_b):
    return pl.pallas_call(
        _mods_kernel,
        grid=(3,),
        in_specs=[
            pl.BlockSpec((8, D_MODEL), lambda j: (0, 0)),
            pl.BlockSpec((D_MODEL, D_MODEL), lambda j: (0, j)),
            pl.BlockSpec((1, D_MODEL), lambda j: (0, j)),
        ],
        out_specs=pl.BlockSpec((8, D_MODEL), lambda j: (0, j)),
        out_shape=jax.ShapeDtypeStruct((8, 3 * D_MODEL), F32),
        compiler_params=pltpu.CompilerParams(dimension_semantics=("arbitrary",), vmem_limit_bytes=VMEM_LIMIT),
        name="mods",
    )(c_all, ada_w, ada_b)


def _in_proj_kernel(xp_ref, x_ref, xn_ref, mods_ref, prew_ref, win_ref, dncw_ref, rgcw_ref, rgcb_ref,
                    alog_ref, dtb_ref,
                    q_ref, k_ref, v_ref, z_ref, xf_ref, gate_ref, gb_ref,
                    h_scr, tri_scr, tab_scr, *, is_lat, mod_row0, n_tiles):
    b = pl.program_id(0)
    i = pl.program_id(1)

    @pl.when((b == 0) & (i == 0))
    def _():
        r = lax.broadcasted_iota(jnp.int32, (TILE, TILE), 0)
        c = lax.broadcasted_iota(jnp.int32, (TILE, TILE), 1)
        tri_scr[...] = jnp.where(((r // CHUNK) == (c // CHUNK)) & (r >= c), 1.0, 0.0).astype(BF16)
        if is_lat:
            tab_scr[...] = _pe_table()

    mod = mods_ref[pl.ds(mod_row0 + (b if is_lat else 0), 1), :]
    shift = mod[:, :D_MODEL]
    scale = mod[:, D_MODEL:2 * D_MODEL]
    prew = prew_ref[...]

    def norm_mod(xv):
        ms = jnp.mean(xv * xv, axis=-1, keepdims=True)
        return (xv * lax.rsqrt(ms + EPS) * prew) * (1.0 + scale) + shift

    x_prev = xp_ref[0]
    x_main = x_ref[0]
    x_next = xn_ref[0]
    if is_lat:
        r0 = i * (TILE // GRID_W)
        x_main = x_main + _pe_tile(tab_scr, i)
        x_prev = x_prev + _pe_rows(tab_scr, jnp.maximum(r0 - 1, 0), HALO, GRID_W - HALO)
        x_next = x_next + _pe_rows(tab_scr, jnp.minimum(r0 + TILE // GRID_W, GRID_W - 1), HALO, 0)

    h_scr[0:HALO, :] = jnp.where(i > 0, norm_mod(x_prev), 0.0).astype(BF16)
    h_scr[HALO:HALO + TILE, :] = norm_mod(x_main).astype(BF16)
    h_scr[HALO + TILE:, :] = jnp.where(i < n_tiles - 1, norm_mod(x_next), 0.0).astype(BF16)

    def conv_seg(col0, cw_ref, cw_col0):
        p = jnp.dot(h_scr[...], win_ref[:, col0:col0 + SEG], preferred_element_type=F32)
        acc = None
        for j in range(CONV_W):
            shifted = p if j == CONV_PAD_L else pltpu.roll(p, (CONV_PAD_L - j) % (TILE + 2 * HALO), 0)
            tap = shifted[HALO:HALO + TILE, :] * cw_ref[j:j + 1, cw_col0:cw_col0 + SEG]
            acc = tap if acc is None else acc + tap
        return acc

    def plain_seg(col0, width):
        return jnp.dot(h_scr[HALO:HALO + TILE, :], win_ref[:, col0:col0 + width], preferred_element_type=F32)

    raw = plain_seg(P_SMALL, 128)
    beta = _sigmoid(raw)
    g = -jnp.exp(alog_ref[...]) * _softplus(raw + dtb_ref[...])
    g_hi = g.astype(BF16)
    g_r1 = g - g_hi.astype(F32)
    g_mid = g_r1.astype(BF16)
    g_lo = (g_r1 - g_mid.astype(F32)).astype(BF16)

    def qkv_seg(s):
        act = _silu(conv_seg(s * SEG, dncw_ref, s * SEG))
        which = (s * SEG) // D_DN
        out_ref = (q_ref, k_ref, v_ref)[which]
        for hl in range(SEG // HEAD_DIM):
            head = ((s * SEG) % D_DN) // HEAD_DIM + hl
            xh = act[:, hl * HEAD_DIM:(hl + 1) * HEAD_DIM]
            if which < 2:
                xh = xh * lax.rsqrt(jnp.sum(xh * xh, axis=-1, keepdims=True) + EPS)
            out_ref[head] = xh

    qkv_seg(0)

    lower = tri_scr[...]
    cum_f = (jnp.dot(lower, g_hi, preferred_element_type=F32) + jnp.dot(lower, g_mid, preferred_element_type=F32)
             + jnp.dot(lower, g_lo, preferred_element_type=F32))
    tot = jnp.concatenate([jnp.broadcast_to(cum_f[ci * CHUNK + CHUNK - 1:(ci + 1) * CHUNK, :], (CHUNK, 128))
                           for ci in range(N_CHUNKS)], axis=0)
    cum_b = tot - cum_f + g
    lane = lax.broadcasted_iota(jnp.int32, (TILE, 128), 1) % GB_LANES
    comb = jnp.where(lane < 2, beta, jnp.where(lane == 2, cum_f, jnp.where(lane == 3, cum_b, 0.0)))
    for h in range(N_HEADS):
        gb_ref[h] = comb[:, h * GB_LANES:(h + 1) * GB_LANES]

    for s in range(1, 3 * D_DN // SEG):
        qkv_seg(s)
    for s in range(D_RG // SEG):
        xf_ref[:, s * SEG:(s + 1) * SEG] = (conv_seg(P_RX + s * SEG, rgcw_ref, s * SEG)
                                            + rgcb_ref[:, s * SEG:(s + 1) * SEG])
    for s in range(D_DN // SEG):
        z_ref[:, s * SEG:(s + 1) * SEG] = plain_seg(P_Z + s * SEG, SEG)
    for s in range(D_RG // SEG):
        gate_ref[:, s * SEG:(s + 1) * SEG] = plain_seg(P_RG + s * SEG, SEG)


def _in_proj(x, mods, prew, win_p, dncw, rgcw, rgcb, alog_row, dtb_row, *, is_lat, mod_row0):
    bsz, t, _ = x.shape
    n_tiles = t // TILE
    n_tok = bsz * t
    hb = TILE // HALO
    tok = lambda b, i: (b * n_tiles + i, 0)
    head_tok = lambda b, i: (0, b * n_tiles + i, 0)
    kern = functools.partial(_in_proj_kernel, is_lat=is_lat, mod_row0=mod_row0, n_tiles=n_tiles)
    return pl.pallas_call(
        kern,
        grid=(bsz, n_tiles),
        in_specs=[
            pl.BlockSpec((1, HALO, D_MODEL), lambda b, i: (b, jnp.maximum(i * hb - 1, 0), 0)),
            pl.BlockSpec((1, TILE, D_MODEL), lambda b, i: (b, i, 0)),
            pl.BlockSpec((1, HALO, D_MODEL), lambda b, i: (b, jnp.minimum((i + 1) * hb, t // HALO - 1), 0)),
            _const_spec((8, 3 * D_MODEL)),
            _const_spec((1, D_MODEL)),
            _const_spec((D_MODEL, D_INP)),
            _const_spec((CONV_W, 3 * D_DN)),
            _const_spec((CONV_W, D_RG)),
            _const_spec((1, D_RG)),
            _const_spec((1, 128)),
            _const_spec((1, 128)),
        ],
        out_specs=[
            pl.BlockSpec((N_HEADS, TILE, HEAD_DIM), head_tok),
            pl.BlockSpec((N_HEADS, TILE, HEAD_DIM), head_tok),
            pl.BlockSpec((N_HEADS, TILE, HEAD_DIM), head_tok),
            pl.BlockSpec((TILE, D_DN), tok),
            pl.BlockSpec((TILE, D_RG), tok),
            pl.BlockSpec((TILE, D_RG), tok),
            pl.BlockSpec((N_HEADS, TILE, GB_LANES), head_tok),
        ],
        out_shape=[
            jax.ShapeDtypeStruct((N_HEADS, n_tok, HEAD_DIM), F32),
            jax.ShapeDtypeStruct((N_HEADS, n_tok, HEAD_DIM), F32),
            jax.ShapeDtypeStruct((N_HEADS, n_tok, HEAD_DIM), F32),
            jax.ShapeDtypeStruct((n_tok, D_DN), F32),
            jax.ShapeDtypeStruct((n_tok, D_RG), F32),
            jax.ShapeDtypeStruct((n_tok, D_RG), F32),
            jax.ShapeDtypeStruct((N_HEADS, n_tok, GB_LANES), F32),
        ],
        scratch_shapes=[
            pltpu.VMEM((TILE + 2 * HALO, D_MODEL), BF16),
            pltpu.VMEM((TILE, TILE), BF16),
            pltpu.VMEM((GRID_W, D_MODEL // 2), F32),
        ],
        compiler_params=pltpu.CompilerParams(dimension_semantics=("arbitrary", "arbitrary"),
                                             vmem_limit_bytes=VMEM_LIMIT),
        name="in_proj_lat" if is_lat else "in_proj_ctx",
    )(x, x, x, mods, prew, win_p, dncw, rgcw, rgcb, alog_row, dtb_row)


K_EYE, K_INCL_F, K_INCL_B, K_STRICT_F, K_STRICT_B, K_LEVEL0 = 0, 1, 2, 3, 4, 5
N_LEVELS = int(math.log2(CHUNK))
N_CAT_CONSTS = K_LEVEL0 + N_LEVELS


def _delta_consts(cst, half):
    r = lax.broadcasted_iota(jnp.int32, (CHUNK, TILE), 0)
    c = lax.broadcasted_iota(jnp.int32, (CHUNK, TILE), 1) % CHUNK
    cst[K_EYE] = jnp.where(r == c, 1.0, 0.0)
    cst[K_INCL_F] = jnp.where(r >= c, 0.0, NEG_BIG)
    cst[K_INCL_B] = jnp.where(r <= c, 0.0, NEG_BIG)
    cst[K_STRICT_F] = jnp.where(r > c, 0.0, NEG_BIG)
    cst[K_STRICT_B] = jnp.where(r < c, 0.0, NEG_BIG)
    for lv in range(N_LEVELS):
        m = 1 << lv
        cst[K_LEVEL0 + lv] = jnp.where(((r // (2 * m)) == (c // (2 * m))) & ((r // m) != (c // m)), 1.0, 0.0)
    lane = lax.broadcasted_iota(jnp.int32, (CHUNK, 128), 1)
    half[0] = jnp.where(lane < CHUNK, 1.0, 0.0).astype(BF16)
    half[1] = jnp.where(lane >= CHUNK, 1.0, 0.0).astype(BF16)


def _cat_of_diag_blocks(full):
    lo = lax.broadcasted_iota(jnp.int32, (CHUNK, 128), 1) < CHUNK
    tiles = []
    for t in range(TILE // 128):
        a = full[(2 * t) * CHUNK:(2 * t + 1) * CHUNK, t * 128:(t + 1) * 128]
        b = full[(2 * t + 1) * CHUNK:(2 * t + 2) * CHUNK, t * 128:(t + 1) * 128]
        tiles.append(jnp.where(lo, a, b))
    return jnp.concatenate(tiles, axis=1)


def _cat_of_columns(col):
    lo = lax.broadcasted_iota(jnp.int32, (CHUNK, 128), 1) < CHUNK
    tiles = []
    for t in range(TILE // 128):
        a = jnp.broadcast_to(col[(2 * t) * CHUNK:(2 * t + 1) * CHUNK, :], (CHUNK, 128))
        b = jnp.broadcast_to(col[(2 * t + 1) * CHUNK:(2 * t + 2) * CHUNK, :], (CHUNK, 128))
        tiles.append(jnp.where(lo, a, b))
    return jnp.concatenate(tiles, axis=1)


def _block_diag(cat, half):
    zero = jnp.zeros((CHUNK, 128), BF16)
    rows = []
    for j in range(N_CHUNKS):
        t = j // 2
        blk = cat[:, t * 128:(t + 1) * 128] * half[j % 2]
        rows.append(jnp.concatenate([blk if tt == t else zero for tt in range(TILE // 128)], axis=1))
    return jnp.concatenate(rows, axis=0)


def _delta_tile(probs, s_ref, cst, half):
    n = len(probs)
    eye8 = jnp.where(lax.broadcasted_iota(jnp.int32, (GB_LANES, GB_LANES), 0)
                     == lax.broadcasted_iota(jnp.int32, (GB_LANES, GB_LANES), 1), 1.0, 0.0)
    beta, gcum, qs, kbeta, attn, ncat = [], [], [], [], [], []
    for q, k, v, gb, hl, d in probs:
        beta.append(gb[:, d:d + 1])
        gcum.append(gb[:, 2 + d:3 + d])
        gb_t = lax.dot_general(eye8, gb, (((1,), (1,)), ((), ())), precision=lax.Precision.HIGHEST,
                               preferred_element_type=F32)
        diff = _cat_of_columns(gcum[-1]) - gb_t[2 + d:3 + d, :]
        qs.append(q * (HEAD_DIM ** -0.5))
        kbeta.append(k * beta[-1])
        prod = lax.dot_general(jnp.concatenate([qs[-1], kbeta[-1]], axis=0).astype(BF16), k.astype(BF16),
                               (((1,), (1,)), ((), ())), preferred_element_type=F32)
        attn.append(_cat_of_diag_blocks(prod[:TILE]) * jnp.exp(diff + cst[K_INCL_B if d else K_INCL_F]))
        ncat.append(_cat_of_diag_blocks(prod[TILE:]) * jnp.exp(diff + cst[K_STRICT_B if d else K_STRICT_F]))

    xcat = [cst[K_EYE] - ncat[p] * cst[K_LEVEL0] for p in range(n)]
    xbd = [_block_diag(xcat[p].astype(BF16), half) for p in range(n)]
    for lv in range(1, N_LEVELS):
        cbd = [_block_diag((ncat[p] * cst[K_LEVEL0 + lv]).astype(BF16), half) for p in range(n)]
        xc = [jnp.dot(xcat[p].astype(BF16), cbd[p], preferred_element_type=F32) for p in range(n)]
        xcat = [xcat[p] - jnp.dot(xc[p].astype(BF16), xbd[p], preferred_element_type=F32) for p in range(n)]
        xbd = [_block_diag(xcat[p].astype(BF16), half) for p in range(n)]

    u, w, qd, kd, tot = [], [], [], [], []
    for p, (q, k, v, gb, hl, d) in enumerate(probs):
        eg = jnp.exp(gcum[p])
        rhs = jnp.concatenate([v * beta[p], kbeta[p] * eg], axis=1).astype(BF16)
        sol = jnp.dot(xbd[p], rhs, preferred_element_type=F32)
        u.append(sol[:, :HEAD_DIM])
        w.append(sol[:, HEAD_DIM:])
        qd.append(qs[p] * eg)
        last = [ci * CHUNK if d else ci * CHUNK + CHUNK - 1 for ci in range(N_CHUNKS)]
        tot.append([gcum[p][r0:r0 + 1, :] for r0 in last])
        gtot = jnp.concatenate([jnp.broadcast_to(x, (CHUNK, 1)) for x in tot[p]], axis=0)
        kd.append(k * jnp.exp(gtot - gcum[p]))

    s = [s_ref[d, hl] for (_, _, _, _, hl, d) in probs]
    vnew = [[None] * N_CHUNKS for _ in range(n)]
    obase = [[None] * N_CHUNKS for _ in range(n)]
    for step in range(N_CHUNKS):
        for p, (_, _, _, _, hl, d) in enumerate(probs):
            ci = N_CHUNKS - 1 - step if d else step
            rows = slice(ci * CHUNK, (ci + 1) * CHUNK)
            ws = _mm(jnp.concatenate([w[p][rows], qd[p][rows]], axis=0), s[p])
            vn = u[p][rows] - ws[:CHUNK]
            obase[p][ci] = ws[CHUNK:]
            vnew[p][ci] = vn.astype(BF16)
            upd = lax.dot_general(kd[p][rows].astype(BF16), vnew[p][ci], (((0,), (0,)), ((), ())),
                                  preferred_element_type=F32)
            s[p] = s[p] * jnp.exp(tot[p][ci]) + upd
    outs = []
    zero = jnp.zeros((CHUNK, HEAD_DIM), BF16)
    for p, (_, _, _, _, hl, d) in enumerate(probs):
        s_ref[d, hl] = s[p]
        vbd = jnp.concatenate([jnp.concatenate([vnew[p][j] if jj == j else zero for jj in range(N_CHUNKS)], axis=1)
                               for j in range(N_CHUNKS)], axis=0)
        ocat = jnp.dot(attn[p].astype(BF16), vbd, preferred_element_type=F32)
        outs.append(jnp.concatenate([obase[p][j] + ocat[:, j * HEAD_DIM:(j + 1) * HEAD_DIM]
                                     for j in range(N_CHUNKS)], axis=0))
    return outs


def _delta_kernel(*refs, has_s0, emit_state, n_tiles):
    if has_s0:
        (qf, kf, vf, gbf, qb, kb, vb, gbb, s0_ref), rest = refs[:9], refs[9:]
    else:
        (qf, kf, vf, gbf, qb, kb, vb, gbb), rest = refs[:8], refs[8:]
        s0_ref = None
    if emit_state:
        of_ref, ob_ref, sout_ref, s_scr, cst, half = rest
    else:
        of_ref, ob_ref, s_scr, cst, half = rest
        sout_ref = None
    i = pl.program_id(2)

    @pl.when((pl.program_id(0) == 0) & (pl.program_id(1) == 0) & (i == 0))
    def _():
        _delta_consts(cst, half)

    @pl.when(i == 0)
    def _():
        if has_s0:
            s_scr[...] = s0_ref[0]
        else:
            s_scr[...] = jnp.zeros_like(s_scr)

    probs = []
    for hl in range(DELTA_HEADS):
        probs.append((qf[hl], kf[hl], vf[hl], gbf[hl], hl, 0))
        probs.append((qb[hl], kb[hl], vb[hl], gbb[hl], hl, 1))
    outs = _delta_tile(probs, s_scr, cst, half)
    for hl in range(DELTA_HEADS):
        of_ref[hl] = outs[2 * hl]
        ob_ref[hl] = outs[2 * hl + 1]

    if emit_state:
        @pl.when(i == n_tiles - 1)
        def _():
            sout_ref[0] = s_scr[...]


def _delta(q, k, v, gb, s0, *, bsz, n_tiles, emit_state):
    n_tok = bsz * n_tiles * TILE
    hb = DELTA_HEADS
    fwd = lambda b, h, i: (h, b * n_tiles + i, 0)
    bwd = lambda b, h, i: (h, b * n_tiles + n_tiles - 1 - i, 0)
    has_s0 = s0 is not None
    in_specs, args = [], []
    for m in (fwd, bwd):
        in_specs += [pl.BlockSpec((hb, TILE, HEAD_DIM), m)] * 3 + [pl.BlockSpec((hb, TILE, GB_LANES), m)]
        args += [q, k, v, gb]
    state_spec = pl.BlockSpec((1, 2, hb, HEAD_DIM, HEAD_DIM), lambda b, h, i: (b, 0, h, 0, 0))
    if has_s0:
        in_specs.append(state_spec)
        args.append(s0)
    out_specs = [pl.BlockSpec((hb, TILE, HEAD_DIM), fwd), pl.BlockSpec((hb, TILE, HEAD_DIM), bwd)]
    out_shape = [jax.ShapeDtypeStruct((N_HEADS, n_tok, HEAD_DIM), F32)] * 2
    if emit_state:
        out_specs.append(state_spec)
        out_shape.append(jax.ShapeDtypeStruct((bsz, 2, N_HEADS, HEAD_DIM, HEAD_DIM), F32))
    kern = functools.partial(_delta_kernel, has_s0=has_s0, emit_state=emit_state, n_tiles=n_tiles)
    return pl.pallas_call(
        kern,
        grid=(bsz, N_HEADS // hb, n_tiles),
        in_specs=in_specs,
        out_specs=out_specs,
        out_shape=out_shape,
        scratch_shapes=[pltpu.VMEM((2, hb, HEAD_DIM, HEAD_DIM), F32), pltpu.VMEM((N_CAT_CONSTS, CHUNK, TILE), F32),
                        pltpu.VMEM((2, CHUNK, 128), BF16)],
        compiler_params=pltpu.CompilerParams(dimension_semantics=("arbitrary", "arbitrary", "arbitrary"),
                                             vmem_limit_bytes=VMEM_LIMIT),
        name="delta_lat" if has_s0 else "delta_ctx",
    )(*args)


RG_SUB = D_RG // 128


def _rglru_gates(xf_ref, w_ref, br_ref, bi_ref, lam_ref, a_scr, b_scr, d):
    for n in range(N_RG_BLOCKS):
        cols = slice(n * RG_BLOCK, (n + 1) * RG_BLOCK)
        xb = xf_ref[:, cols]
        ri = jnp.dot(xb.astype(BF16), w_ref[d, n], preferred_element_type=F32)
        tr = jnp.tanh(0.5 * ri[:, :RG_BLOCK] + 0.5 * br_ref[d:d + 1, cols])
        ti = jnp.tanh(0.5 * ri[:, RG_BLOCK:] + 0.5 * bi_ref[d:d + 1, cols])
        c2 = (-0.5 * RG_C * LOG2_E) * _softplus(-lam_ref[d:d + 1, cols])
        log2_a = c2 * tr + c2
        a = jnp.exp2(log2_a)
        one_minus_a2 = jnp.tanh((-LN_2) * log2_a) * (a * a + 1.0)
        gain = jnp.where(one_minus_a2 > 0.0, one_minus_a2 * lax.rsqrt(one_minus_a2), 0.0)
        half_x = 0.5 * xb
        a_scr[pl.ds(n, TILE, stride=RG_SUB), :] = a
        b_scr[pl.ds(n, TILE, stride=RG_SUB), :] = gain * (half_x * ti + half_x)


def _rglru_kernel(*refs, has_h0, emit_state, n_tiles):
    if has_h0:
        (xff, xfb, h0_ref), rest = refs[:3], refs[3:]
    else:
        (xff, xfb), rest = refs[:2], refs[2:]
        h0_ref = None
    w_ref, br_ref, bi_ref, lam_ref = rest[:4]
    rest = rest[4:]
    if emit_state:
        hf_ref, hb_ref, sfin_ref, carry, af_scr, bf_scr, ab_scr, bb_scr = rest
    else:
        hf_ref, hb_ref, carry, af_scr, bf_scr, ab_scr, bb_scr = rest
        sfin_ref = None
    i = pl.program_id(1)

    @pl.when(i == 0)
    def _():
        if has_h0:
            carry[...] = h0_ref[0]
        else:
            carry[...] = jnp.zeros_like(carry)

    _rglru_gates(xff, w_ref, br_ref, bi_ref, lam_ref, af_scr, bf_scr, 0)
    _rglru_gates(xfb, w_ref, br_ref, bi_ref, lam_ref, ab_scr, bb_scr, 1)

    def body(g, hs):
        hf, hb = hs
        for r in range(8):
            tf = g * 8 + r
            rows = pl.ds(pl.multiple_of(tf * RG_SUB, RG_SUB), RG_SUB)
            hf = af_scr[rows, :] * hf + bf_scr[rows, :]
            hf_ref[rows, :] = hf
            tb = TILE - 1 - tf
            rows = pl.ds(pl.multiple_of(tb * RG_SUB, RG_SUB), RG_SUB)
            hb = ab_scr[rows, :] * hb + bb_scr[rows, :]
            hb_ref[rows, :] = hb
        return hf, hb

    hf, hb = lax.fori_loop(0, TILE // 8, body, (carry[0], carry[1]))
    carry[0] = hf
    carry[1] = hb

    if emit_state:
        @pl.when(i == n_tiles - 1)
        def _():
            sfin_ref[0] = carry[...]


def _rglru(xf, h0, wcat, br, bi, lam, *, bsz, n_tiles, emit_state):
    n_tok = bsz * n_tiles * TILE
    fwd = lambda b, i: (b * n_tiles + i, 0)
    bwd = lambda b, i: (b * n_tiles + n_tiles - 1 - i, 0)
    state_spec = pl.BlockSpec((1, 2, RG_SUB, 128), lambda b, i: (b, 0, 0, 0))
    has_h0 = h0 is not None
    in_specs = [pl.BlockSpec((TILE, D_RG), fwd), pl.BlockSpec((TILE, D_RG), bwd)]
    args = [xf, xf]
    if has_h0:
        in_specs.append(state_spec)
        args.append(h0.reshape(bsz, 2, RG_SUB, 128))
    in_specs += [_const_spec((2, N_RG_BLOCKS, RG_BLOCK, 2 * RG_BLOCK)), _const_spec((2, D_RG)),
                 _const_spec((2, D_RG)), _const_spec((2, D_RG))]
    args += [wcat, br, bi, lam]
    out_specs = [pl.BlockSpec((TILE * RG_SUB, 128), fwd), pl.BlockSpec((TILE * RG_SUB, 128), bwd)]
    out_shape = [jax.ShapeDtypeStruct((n_tok * RG_SUB, 128), F32)] * 2
    if emit_state:
        out_specs.append(state_spec)
        out_shape.append(jax.ShapeDtypeStruct((bsz, 2, RG_SUB, 128), F32))
    kern = functools.partial(_rglru_kernel, has_h0=has_h0, emit_state=emit_state, n_tiles=n_tiles)
    return pl.pallas_call(
        kern,
        grid=(bsz, n_tiles),
        in_specs=in_specs,
        out_specs=out_specs,
        out_shape=out_shape,
        scratch_shapes=[pltpu.VMEM((2, RG_SUB, 128), F32)] + [pltpu.VMEM((TILE * RG_SUB, 128), F32)] * 4,
        compiler_params=pltpu.CompilerParams(dimension_semantics=("arbitrary", "arbitrary"),
                                             vmem_limit_bytes=VMEM_LIMIT),
        name="rglru_lat" if has_h0 else "rglru_ctx",
    )(*args)


def _out_proj_kernel(x_ref, of_ref, ob_ref, z_ref, hf_ref, hb_ref, gate_ref, mods_ref, dnw_ref, postw_ref,
                     wout_ref, y_ref, mix_scr, tab_scr, *, is_lat, mod_row0, blocks_per_seq):
    i = pl.program_id(0)
    if is_lat:
        @pl.when(i == 0)
        def _():
            tab_scr[...] = _pe_table()

    dnw = dnw_ref[...]
    for h in range(N_HEADS):
        cols = slice(h * HEAD_DIM, (h + 1) * HEAD_DIM)
        o = of_ref[h] + ob_ref[h]
        on = o * lax.rsqrt(jnp.mean(o * o, axis=-1, keepdims=True) + EPS) * dnw
        mix_scr[:, cols] = (on * _silu(z_ref[:, cols])).astype(BF16)
    for n in range(RG_SUB):
        cols = slice(n * 128, (n + 1) * 128)
        rows = pl.ds(n, OUT_ROWS, stride=RG_SUB)
        mix_scr[:, D_DN + n * 128:D_DN + (n + 1) * 128] = (
            (hf_ref[rows, :] + hb_ref[rows, :]) * _silu(gate_ref[:, cols])).astype(BF16)

    y = jnp.dot(mix_scr[...], wout_ref[...], preferred_element_type=F32)
    yn = y * lax.rsqrt(jnp.mean(y * y, axis=-1, keepdims=True) + EPS) * postw_ref[...]
    mod = mods_ref[pl.ds(mod_row0 + (i // blocks_per_seq if is_lat else 0), 1), :]
    xs = x_ref[...]
    if is_lat:
        first_tile = (i % blocks_per_seq) * (OUT_ROWS // TILE)
        xs = xs + jnp.concatenate([_pe_tile(tab_scr, first_tile + j) for j in range(OUT_ROWS // TILE)], axis=0)
    y_ref[...] = xs + mod[:, 2 * D_MODEL:] * yn


def _out_proj(x, o_f, o_b, z, h_f, h_b, gate, mods, dnw, postw, wout, *, is_lat, mod_row0):
    bsz, t, _ = x.shape
    n_tok = bsz * t
    assert n_tok % OUT_ROWS == 0 and (t % OUT_ROWS == 0 or not is_lat)
    tok = lambda i: (i, 0)
    head_tok = lambda i: (0, i, 0)
    kern = functools.partial(_out_proj_kernel, is_lat=is_lat, mod_row0=mod_row0, blocks_per_seq=max(t // OUT_ROWS, 1))
    y = pl.pallas_call(
        kern,
        grid=(n_tok // OUT_ROWS,),
        in_specs=[
            pl.BlockSpec((OUT_ROWS, D_MODEL), tok),
            pl.BlockSpec((N_HEADS, OUT_ROWS, HEAD_DIM), head_tok),
            pl.BlockSpec((N_HEADS, OUT_ROWS, HEAD_DIM), head_tok),
            pl.BlockSpec((OUT_ROWS, D_DN), tok),
            pl.BlockSpec((OUT_ROWS * RG_SUB, 128), tok),
            pl.BlockSpec((OUT_ROWS * RG_SUB, 128), tok),
            pl.BlockSpec((OUT_ROWS, D_RG), tok),
            _const_spec((8, 3 * D_MODEL)),
            _const_spec((1, HEAD_DIM)),
            _const_spec((1, D_MODEL)),
            _const_spec((D_DN + D_RG, D_MODEL)),
        ],
        out_specs=pl.BlockSpec((OUT_ROWS, D_MODEL), tok),
        out_shape=jax.ShapeDtypeStruct((n_tok, D_MODEL), F32),
        scratch_shapes=[pltpu.VMEM((OUT_ROWS, D_DN + D_RG), BF16), pltpu.VMEM((GRID_W, D_MODEL // 2), F32)],
        compiler_params=pltpu.CompilerParams(dimension_semantics=("arbitrary",), vmem_limit_bytes=VMEM_LIMIT),
        name="out_proj_lat" if is_lat else "out_proj_ctx",
    )(x.reshape(n_tok, D_MODEL), o_f, o_b, z, h_f, h_b, gate, mods, dnw, postw, wout)
    return y.reshape(bsz, t, D_MODEL)


def _permute_small(w_cols):
    lead = w_cols.shape[:-1]
    ba = w_cols.reshape(lead + (2, 2, N_HEADS))
    per_head = jnp.moveaxis(ba, -1, -3).reshape(lead + (N_HEADS, 4))
    per_head = jnp.pad(per_head, [(0, 0)] * len(lead) + [(0, 0), (0, GB_LANES - 4)])
    flat = per_head.reshape(lead + (N_HEADS * GB_LANES,))
    return jnp.pad(flat, [(0, 0)] * len(lead) + [(0, 128 - N_HEADS * GB_LANES)])


def _mixer_path(x, mods, weights, s_dn0, s_rg0, *, is_lat, mod_row0):
    (prew, postw, win_p, wout, dncw, alog_row, dtb_row, dnw, rgcw, rgcb, wcat, br, bi, lam) = weights
    bsz, t, _ = x.shape
    n_tiles = t // TILE
    q, k, v, z, xf, gate, gb = _in_proj(x, mods, prew, win_p, dncw, rgcw, rgcb, alog_row, dtb_row,
                                        is_lat=is_lat, mod_row0=mod_row0)
    emit = not is_lat
    dn = _delta(q, k, v, gb, s_dn0, bsz=bsz, n_tiles=n_tiles, emit_state=emit)
    rg = _rglru(xf, s_rg0, wcat, br, bi, lam, bsz=bsz, n_tiles=n_tiles, emit_state=emit)
    y = _out_proj(x, dn[0], dn[1], z, rg[0], rg[1], gate, mods, dnw, postw, wout,
                  is_lat=is_lat, mod_row0=mod_row0)
    if emit:
        return y, dn[2], rg[2].reshape(bsz, 2, D_RG)
    return y


def kernel(x_prompt, x_sample, state_delta, state_rglru, c, c_ctx, ada_w, ada_b, pre_norm_w, post_norm_w, w_in, w_out, dn_conv_w, dn_a_log, dn_dt_bias, dn_norm_w, rg_conv_w, rg_conv_b, rg_w_r, rg_b_r, rg_w_i, rg_b_i, rg_lam):
    l = 0
    n_lat = c.shape[0]
    c_all = jnp.concatenate([c_ctx[None], c, jnp.zeros((8 - 1 - n_lat, D_MODEL), F32)], axis=0)
    mods = _mods(c_all, ada_w[l], ada_b[l][None])

    w = w_in[l].astype(BF16)
    win_p = jnp.concatenate([w[:, :OFF_B], w[:, OFF_RX:], _permute_small(w[:, OFF_B:OFF_RX])], axis=1)
    zeros16 = jnp.zeros((2 * N_HEADS,), F32)
    alog_row = _permute_small(jnp.concatenate([zeros16, dn_a_log[l].reshape(-1)]))[None]
    dtb_row = _permute_small(jnp.concatenate([zeros16, dn_dt_bias[l].reshape(-1)]))[None]
    wcat = jnp.concatenate([rg_w_r[l], rg_w_i[l]], axis=-1).astype(BF16)
    weights = (pre_norm_w[l][None], post_norm_w[l][None], win_p, w_out[l].astype(BF16), dn_conv_w[l],
               alog_row, dtb_row, dn_norm_w[l][None], rg_conv_w[l], rg_conv_b[l][None], wcat,
               rg_b_r[l], rg_b_i[l], rg_lam[l])

    y_prompt, s_dn, s_rg = _mixer_path(x_prompt, mods, weights, None, None, is_lat=False, mod_row0=0)
    y_sample = _mixer_path(x_sample, mods, weights, state_delta[:, l], state_rglru[:, l], is_lat=True, mod_row0=1)
    return (y_prompt, y_sample, s_dn[:, None].astype(x_prompt.dtype), s_rg[:, None].astype(x_prompt.dtype))
```

```python
import functools
import math

import jax
import jax.numpy as jnp
from jax import lax
from jax.experimental import pallas as pl
from jax.experimental.pallas import tpu as pltpu

F32 = jnp.float32
BF16 = jnp.bfloat16

D_MODEL = 1024
N_HEADS = 8
HEAD_DIM = 128
D_DN = N_HEADS * HEAD_DIM
D_RG = 1024
N_RG_BLOCKS = 8
RG_BLOCK = D_RG // N_RG_BLOCKS
CONV_W = 4
CONV_PAD_L = 2
CHUNK = 64
RG_C = 8.0
EPS = 1e-6
GRID_W = 64
OFF_Z = 3 * D_DN
OFF_B = 4 * D_DN
OFF_A = OFF_B + 2 * N_HEADS
OFF_RX = OFF_A + 2 * N_HEADS
D_IN = OFF_RX + 2 * D_RG

TILE = 256
HALO = 16
N_CHUNKS = TILE // CHUNK
SEG = 512
GB_LANES = 8
DELTA_HEADS = 8
OUT_ROWS = 512
IN_ROWS = 512
PLAIN = 256
MIX = SEG + PLAIN
N_QKV_SEGS = 3 * D_DN // SEG
N_MIX_SEGS = N_QKV_SEGS + D_RG // SEG
assert N_MIX_SEGS * PLAIN == D_DN + D_RG
P_SMALL = N_MIX_SEGS * MIX
D_INP = P_SMALL + 128
NEG_BIG = -1e30
LOG2_E = 1.4426950408889634
LN_2 = 0.6931471805599453
VMEM_LIMIT = 56 * 1024 * 1024


def _sigmoid(x):
    return 0.5 * jnp.tanh(0.5 * x) + 0.5


def _silu(x):
    h = 0.5 * x
    return h * jnp.tanh(h) + h


def _softplus(x):
    return jnp.maximum(x, 0.0) + jnp.log1p(jnp.exp(-jnp.abs(x)))


def _mm(a, b):
    return jnp.dot(a.astype(BF16), b.astype(BF16), preferred_element_type=F32)


def _const_spec(shape):
    nd = len(shape)
    return pl.BlockSpec(shape, lambda *_: (0,) * nd, pipeline_mode=pl.Buffered(1))


def _pe_table():
    quarter = D_MODEL // 4
    j = lax.broadcasted_iota(jnp.int32, (GRID_W, quarter), 0).astype(F32)
    kf = lax.broadcasted_iota(jnp.int32, (GRID_W, quarter), 1).astype(F32)
    freqs = jnp.exp(-math.log(10000.0) * kf / quarter)
    ang = j * freqs
    return jnp.concatenate([jnp.sin(ang), jnp.cos(ang)], axis=1)


def _pe_rows(tab_ref, grid_row, n_rows, col0):
    row_part = jnp.broadcast_to(tab_ref[pl.ds(grid_row, 1), :], (n_rows, D_MODEL // 2))
    col_part = tab_ref[col0:col0 + n_rows, :]
    return jnp.concatenate([row_part, col_part], axis=1)


def _pe_tile(tab_ref, i):
    r0 = i * (TILE // GRID_W)
    return jnp.concatenate([_pe_rows(tab_ref, r0 + j, GRID_W, 0) for j in range(TILE // GRID_W)], axis=0)


def _mods_kernel(c_ref, w_ref, b_ref, o_ref):
    o_ref[...] = _mm(_silu(c_ref[...]), w_ref[...]) + b_ref[...]


def _mods(c_all, ada_w, ada_b):
    return pl.pallas_call(
        _mods_kernel,
        grid=(3,),
        in_specs=[
            pl.BlockSpec((8, D_MODEL), lambda j: (0, 0)),
            pl.BlockSpec((D_MODEL, D_MODEL), lambda j: (0, j)),
            pl.BlockSpec((1, D_MODEL), lambda j: (0, j)),
        ],
        out_specs=pl.BlockSpec((8, D_MODEL), lambda j: (0, j)),
        out_shape=jax.ShapeDtypeStruct((8, 3 * D_MODEL), F32),
        compiler_params=pltpu.CompilerParams(dimension_semantics=("arbitrary",), vmem_limit_bytes=VMEM_LIMIT),
        name="mods",
    )(c_all, ada_w, ada_b)


def _in_proj_kernel(xp_ref, x_ref, xn_ref, mods_ref, prew_ref, win_ref, dncw_ref, rgcw_ref, rgcb_ref,
                    alog_ref, dtb_ref,
                    q_ref, k_ref, v_ref, z_ref, xf_ref, gate_ref, gb_ref,
                    h_scr, tri_scr, tab_scr, *, is_lat, mod_row0, seq_len):
    i = pl.program_id(0)
    span = min(seq_len, IN_ROWS)
    n_span = IN_ROWS // span
    blocks_per_seq = max(seq_len // IN_ROWS, 1)
    pitch = span + HALO
    m_rows = HALO + n_span * pitch

    @pl.when(i == 0)
    def _():
        r = lax.broadcasted_iota(jnp.int32, (TILE, TILE), 0)
        c = lax.broadcasted_iota(jnp.int32, (TILE, TILE), 1)
        tri_scr[...] = jnp.where(((r // CHUNK) == (c // CHUNK)) & (r >= c), 1.0, 0.0).astype(BF16)
        if is_lat:
            tab_scr[...] = _pe_table()

    mod = mods_ref[pl.ds(mod_row0 + (i // blocks_per_seq if is_lat else 0), 1), :]
    shift = mod[:, :D_MODEL]
    scale = mod[:, D_MODEL:2 * D_MODEL]
    prew = prew_ref[...]

    def norm_mod(xv):
        ms = jnp.mean(xv * xv, axis=-1, keepdims=True)
        return (xv * lax.rsqrt(ms + EPS) * prew) * (1.0 + scale) + shift

    zero_halo = jnp.zeros((HALO, D_MODEL), BF16)
    if is_lat:
        j = i % blocks_per_seq
        r0 = j * (IN_ROWS // GRID_W)
        x_prev = xp_ref[...] + _pe_rows(tab_scr, jnp.maximum(r0 - 1, 0), HALO, GRID_W - HALO)
        x_next = xn_ref[...] + _pe_rows(tab_scr, jnp.minimum(r0 + IN_ROWS // GRID_W, GRID_W - 1), HALO, 0)
        h_scr[0:HALO, :] = jnp.where(j > 0, norm_mod(x_prev), 0.0).astype(BF16)
        h_scr[HALO + span:, :] = jnp.where(j < blocks_per_seq - 1, norm_mod(x_next), 0.0).astype(BF16)
        for t in range(IN_ROWS // TILE):
            xt = x_ref[t * TILE:(t + 1) * TILE, :] + _pe_tile(tab_scr, j * (IN_ROWS // TILE) + t)
            h_scr[HALO + t * TILE:HALO + (t + 1) * TILE, :] = norm_mod(xt).astype(BF16)
    else:
        h_scr[0:HALO, :] = zero_halo
        for sp in range(n_span):
            h_scr[HALO + sp * pitch:HALO + sp * pitch + span, :] = norm_mod(
                x_ref[sp * span:(sp + 1) * span, :]).astype(BF16)
            h_scr[HALO + sp * pitch + span:HALO + (sp + 1) * pitch, :] = zero_halo

    def spans_of(full):
        parts = [full[HALO + sp * pitch:HALO + sp * pitch + span, :] for sp in range(n_span)]
        return parts[0] if n_span == 1 else jnp.concatenate(parts, axis=0)

    def project(col0, width):
        return jnp.dot(h_scr[...], win_ref[:, col0:col0 + width], preferred_element_type=F32)

    def conv(p, cw_ref, cw_col0):
        acc = None
        for j in range(CONV_W):
            shifted = p if j == CONV_PAD_L else pltpu.roll(p, (CONV_PAD_L - j) % m_rows, 0)
            tap = spans_of(shifted) * cw_ref[j:j + 1, cw_col0:cw_col0 + SEG]
            acc = tap if acc is None else acc + tap
        return acc

    def finish_small(_, p):
        raw = spans_of(p)
        beta = _sigmoid(raw)
        g = -jnp.exp(alog_ref[...]) * _softplus(raw + dtb_ref[...])
        g_hi = g.astype(BF16)
        g_r1 = g - g_hi.astype(F32)
        g_mid = g_r1.astype(BF16)
        g_lo = (g_r1 - g_mid.astype(F32)).astype(BF16)
        lower = tri_scr[...]
        cum_f = jnp.concatenate([
            jnp.dot(lower, g_hi[t * TILE:(t + 1) * TILE], preferred_element_type=F32)
            + jnp.dot(lower, g_mid[t * TILE:(t + 1) * TILE], preferred_element_type=F32)
            + jnp.dot(lower, g_lo[t * TILE:(t + 1) * TILE], preferred_element_type=F32)
            for t in range(IN_ROWS // TILE)], axis=0)
        tot = jnp.concatenate([jnp.broadcast_to(cum_f[ci * CHUNK + CHUNK - 1:(ci + 1) * CHUNK, :], (CHUNK, 128))
                               for ci in range(IN_ROWS // CHUNK)], axis=0)
        cum_b = tot - cum_f + g
        lane = lax.broadcasted_iota(jnp.int32, (IN_ROWS, 128), 1) % GB_LANES
        comb = jnp.where(lane < 2, beta, jnp.where(lane == 2, cum_f, jnp.where(lane == 3, cum_b, 0.0)))
        for h in range(N_HEADS):
            gb_ref[h] = comb[:, h * GB_LANES:(h + 1) * GB_LANES]

    def finish_mixed(s, p):
        if s < N_QKV_SEGS:
            act = _silu(conv(p[:, :SEG], dncw_ref, s * SEG))
            which = (s * SEG) // D_DN
            out_ref = (q_ref, k_ref, v_ref)[which]
            for hl in range(SEG // HEAD_DIM):
                head = ((s * SEG) % D_DN) // HEAD_DIM + hl
                xh = act[:, hl * HEAD_DIM:(hl + 1) * HEAD_DIM]
                if which < 2:
                    xh = xh * lax.rsqrt(jnp.sum(xh * xh, axis=-1, keepdims=True) + EPS)
                out_ref[head] = xh
        else:
            c0 = (s - N_QKV_SEGS) * SEG
            xf_ref[:, c0:c0 + SEG] = conv(p[:, :SEG], rgcw_ref, c0) + rgcb_ref[:, c0:c0 + SEG]
        plain = spans_of(p[:, SEG:])
        if s < D_DN // PLAIN:
            z_ref[:, s * PLAIN:(s + 1) * PLAIN] = plain
        else:
            c0 = (s - D_DN // PLAIN) * PLAIN
            gate_ref[:, c0:c0 + PLAIN] = plain

    segs = [(P_SMALL, 128, finish_small, 0)] + [(s * MIX, MIX, finish_mixed, s) for s in range(N_MIX_SEGS)]
    p_next = project(segs[0][0], segs[0][1])
    for idx, (_, _, finish, s) in enumerate(segs):
        p_cur = p_next
        if idx + 1 < len(segs):
            p_next = project(segs[idx + 1][0], segs[idx + 1][1])
        finish(s, p_cur)


def _in_proj(x, mods, prew, win_p, dncw, rgcw, rgcb, alog_row, dtb_row, *, is_lat, mod_row0):
    bsz, t, _ = x.shape
    n_tok = bsz * t
    assert n_tok % IN_ROWS == 0 and (t % IN_ROWS == 0 if is_lat else IN_ROWS % t == 0)
    span = min(t, IN_ROWS)
    m_rows = HALO + (IN_ROWS // span) * (span + HALO)
    hb = IN_ROWS // HALO
    tok = lambda i: (i, 0)
    head_tok = lambda i: (0, i, 0)
    kern = functools.partial(_in_proj_kernel, is_lat=is_lat, mod_row0=mod_row0, seq_len=t)
    xflat = x.reshape(n_tok, D_MODEL)
    return pl.pallas_call(
        kern,
        grid=(n_tok // IN_ROWS,),
        in_specs=[
            pl.BlockSpec((HALO, D_MODEL), lambda i: (jnp.maximum(i * hb - 1, 0), 0)),
            pl.BlockSpec((IN_ROWS, D_MODEL), tok),
            pl.BlockSpec((HALO, D_MODEL), lambda i: (jnp.minimum((i + 1) * hb, n_tok // HALO - 1), 0)),
            _const_spec((8, 3 * D_MODEL)),
            _const_spec((1, D_MODEL)),
            _const_spec((D_MODEL, D_INP)),
            _const_spec((CONV_W, 3 * D_DN)),
            _const_spec((CONV_W, D_RG)),
            _const_spec((1, D_RG)),
            _const_spec((1, 128)),
            _const_spec((1, 128)),
        ],
        out_specs=[
            pl.BlockSpec((N_HEADS, IN_ROWS, HEAD_DIM), head_tok),
            pl.BlockSpec((N_HEADS, IN_ROWS, HEAD_DIM), head_tok),
            pl.BlockSpec((N_HEADS, IN_ROWS, HEAD_DIM), head_tok),
            pl.BlockSpec((IN_ROWS, D_DN), tok),
            pl.BlockSpec((IN_ROWS, D_RG), tok),
            pl.BlockSpec((IN_ROWS, D_RG), tok),
            pl.BlockSpec((N_HEADS, IN_ROWS, GB_LANES), head_tok),
        ],
        out_shape=[
            jax.ShapeDtypeStruct((N_HEADS, n_tok, HEAD_DIM), F32),
            jax.ShapeDtypeStruct((N_HEADS, n_tok, HEAD_DIM), F32),
            jax.ShapeDtypeStruct((N_HEADS, n_tok, HEAD_DIM), F32),
            jax.ShapeDtypeStruct((n_tok, D_DN), F32),
            jax.ShapeDtypeStruct((n_tok, D_RG), F32),
            jax.ShapeDtypeStruct((n_tok, D_RG), F32),
            jax.ShapeDtypeStruct((N_HEADS, n_tok, GB_LANES), F32),
        ],
        scratch_shapes=[
            pltpu.VMEM((m_rows, D_MODEL), BF16),
            pltpu.VMEM((TILE, TILE), BF16),
            pltpu.VMEM((GRID_W, D_MODEL // 2), F32),
        ],
        compiler_params=pltpu.CompilerParams(dimension_semantics=("arbitrary",), vmem_limit_bytes=VMEM_LIMIT),
        name="in_proj_lat" if is_lat else "in_proj_ctx",
    )(xflat, xflat, xflat, mods, prew, win_p, dncw, rgcw, rgcb, alog_row, dtb_row)


K_EYE, K_INCL_F, K_INCL_B, K_STRICT_F, K_STRICT_B, K_LEVEL0 = 0, 1, 2, 3, 4, 5
N_LEVELS = int(math.log2(CHUNK))
N_CAT_CONSTS = K_LEVEL0 + N_LEVELS


def _delta_consts(cst, half):
    r = lax.broadcasted_iota(jnp.int32, (CHUNK, TILE), 0)
    c = lax.broadcasted_iota(jnp.int32, (CHUNK, TILE), 1) % CHUNK
    cst[K_EYE] = jnp.where(r == c, 1.0, 0.0)
    cst[K_INCL_F] = jnp.where(r >= c, 0.0, NEG_BIG)
    cst[K_INCL_B] = jnp.where(r <= c, 0.0, NEG_BIG)
    cst[K_STRICT_F] = jnp.where(r > c, 0.0, NEG_BIG)
    cst[K_STRICT_B] = jnp.where(r < c, 0.0, NEG_BIG)
    for lv in range(N_LEVELS):
        m = 1 << lv
        cst[K_LEVEL0 + lv] = jnp.where(((r // (2 * m)) == (c // (2 * m))) & ((r // m) != (c // m)), 1.0, 0.0)
    lane = lax.broadcasted_iota(jnp.int32, (CHUNK, 128), 1)
    half[0] = jnp.where(lane < CHUNK, 1.0, 0.0).astype(BF16)
    half[1] = jnp.where(lane >= CHUNK, 1.0, 0.0).astype(BF16)


def _cat_of_diag_blocks(full):
    lo = lax.broadcasted_iota(jnp.int32, (CHUNK, 128), 1) < CHUNK
    tiles = []
    for t in range(TILE // 128):
        a = full[(2 * t) * CHUNK:(2 * t + 1) * CHUNK, t * 128:(t + 1) * 128]
        b = full[(2 * t + 1) * CHUNK:(2 * t + 2) * CHUNK, t * 128:(t + 1) * 128]
        tiles.append(jnp.where(lo, a, b))
    return jnp.concatenate(tiles, axis=1)


def _cat_of_columns(col):
    lo = lax.broadcasted_iota(jnp.int32, (CHUNK, 128), 1) < CHUNK
    tiles = []
    for t in range(TILE // 128):
        a = jnp.broadcast_to(col[(2 * t) * CHUNK:(2 * t + 1) * CHUNK, :], (CHUNK, 128))
        b = jnp.broadcast_to(col[(2 * t + 1) * CHUNK:(2 * t + 2) * CHUNK, :], (CHUNK, 128))
        tiles.append(jnp.where(lo, a, b))
    return jnp.concatenate(tiles, axis=1)


def _block_diag(cat, half):
    zero = jnp.zeros((CHUNK, 128), BF16)
    rows = []
    for j in range(N_CHUNKS):
        t = j // 2
        blk = cat[:, t * 128:(t + 1) * 128] * half[j % 2]
        rows.append(jnp.concatenate([blk if tt == t else zero for tt in range(TILE // 128)], axis=1))
    return jnp.concatenate(rows, axis=0)


def _delta_tile(probs, s_ref, cst, half):
    n = len(probs)
    eye8 = jnp.where(lax.broadcasted_iota(jnp.int32, (GB_LANES, GB_LANES), 0)
                     == lax.broadcasted_iota(jnp.int32, (GB_LANES, GB_LANES), 1), 1.0, 0.0)
    beta, gcum, qs, kbeta, attn, ncat = [], [], [], [], [], []
    for q, k, v, gb, hl, d in probs:
        beta.append(gb[:, d:d + 1])
        gcum.append(gb[:, 2 + d:3 + d])
        gb_t = lax.dot_general(eye8, gb, (((1,), (1,)), ((), ())), precision=lax.Precision.HIGHEST,
                               preferred_element_type=F32)
        diff = _cat_of_columns(gcum[-1]) - gb_t[2 + d:3 + d, :]
        qs.append(q * (HEAD_DIM ** -0.5))
        kbeta.append(k * beta[-1])
        prod = lax.dot_general(jnp.concatenate([qs[-1], kbeta[-1]], axis=0).astype(BF16), k.astype(BF16),
                               (((1,), (1,)), ((), ())), preferred_element_type=F32)
        attn.append(_cat_of_diag_blocks(prod[:TILE]) * jnp.exp(diff + cst[K_INCL_B if d else K_INCL_F]))
        ncat.append(_cat_of_diag_blocks(prod[TILE:]) * jnp.exp(diff + cst[K_STRICT_B if d else K_STRICT_F]))

    xcat = [cst[K_EYE] - ncat[p] * cst[K_LEVEL0] for p in range(n)]
    xbd = [_block_diag(xcat[p].astype(BF16), half) for p in range(n)]
    for lv in range(1, N_LEVELS):
        cbd = [_block_diag((ncat[p] * cst[K_LEVEL0 + lv]).astype(BF16), half) for p in range(n)]
        xc = [jnp.dot(xcat[p].astype(BF16), cbd[p], preferred_element_type=F32) for p in range(n)]
        xcat = [xcat[p] - jnp.dot(xc[p].astype(BF16), xbd[p], preferred_element_type=F32) for p in range(n)]
        xbd = [_block_diag(xcat[p].astype(BF16), half) for p in range(n)]

    u, w, qd, kd, tot = [], [], [], [], []
    for p, (q, k, v, gb, hl, d) in enumerate(probs):
        eg = jnp.exp(gcum[p])
        rhs = jnp.concatenate([v * beta[p], kbeta[p] * eg], axis=1).astype(BF16)
        sol = jnp.dot(xbd[p], rhs, preferred_element_type=F32)
        u.append(sol[:, :HEAD_DIM])
        w.append(sol[:, HEAD_DIM:])
        qd.append(qs[p] * eg)
        last = [ci * CHUNK if d else ci * CHUNK + CHUNK - 1 for ci in range(N_CHUNKS)]
        tot.append([gcum[p][r0:r0 + 1, :] for r0 in last])
        gtot = jnp.concatenate([jnp.broadcast_to(x, (CHUNK, 1)) for x in tot[p]], axis=0)
        kd.append(k * jnp.exp(gtot - gcum[p]))

    s = [s_ref[d, hl] for (_, _, _, _, hl, d) in probs]
    vnew = [[None] * N_CHUNKS for _ in range(n)]
    obase = [[None] * N_CHUNKS for _ in range(n)]
    for step in range(N_CHUNKS):
        for p, (_, _, _, _, hl, d) in enumerate(probs):
            ci = N_CHUNKS - 1 - step if d else step
            rows = slice(ci * CHUNK, (ci + 1) * CHUNK)
            ws = _mm(jnp.concatenate([w[p][rows], qd[p][rows]], axis=0), s[p])
            vn = u[p][rows] - ws[:CHUNK]
            obase[p][ci] = ws[CHUNK:]
            vnew[p][ci] = vn.astype(BF16)
            upd = lax.dot_general(kd[p][rows].astype(BF16), vnew[p][ci], (((0,), (0,)), ((), ())),
                                  preferred_element_type=F32)
            s[p] = s[p] * jnp.exp(tot[p][ci]) + upd
    outs = []
    zero = jnp.zeros((CHUNK, HEAD_DIM), BF16)
    for p, (_, _, _, _, hl, d) in enumerate(probs):
        s_ref[d, hl] = s[p]
        vbd = jnp.concatenate([jnp.concatenate([vnew[p][j] if jj == j else zero for jj in range(N_CHUNKS)], axis=1)
                               for j in range(N_CHUNKS)], axis=0)
        ocat = jnp.dot(attn[p].astype(BF16), vbd, preferred_element_type=F32)
        outs.append(jnp.concatenate([obase[p][j] + ocat[:, j * HEAD_DIM:(j + 1) * HEAD_DIM]
                                     for j in range(N_CHUNKS)], axis=0))
    return outs


def _delta_kernel(*refs, has_s0, emit_state, n_tiles):
    if has_s0:
        (qf, kf, vf, gbf, qb, kb, vb, gbb, s0_ref), rest = refs[:9], refs[9:]
    else:
        (qf, kf, vf, gbf, qb, kb, vb, gbb), rest = refs[:8], refs[8:]
        s0_ref = None
    if emit_state:
        of_ref, ob_ref, sout_ref, s_scr, cst, half = rest
    else:
        of_ref, ob_ref, s_scr, cst, half = rest
        sout_ref = None
    i = pl.program_id(2)

    @pl.when((pl.program_id(0) == 0) & (pl.program_id(1) == 0) & (i == 0))
    def _():
        _delta_consts(cst, half)

    @pl.when(i == 0)
    def _():
        if has_s0:
            s_scr[...] = s0_ref[0]
        else:
            s_scr[...] = jnp.zeros_like(s_scr)

    probs = []
    for hl in range(DELTA_HEADS):
        probs.append((qf[hl], kf[hl], vf[hl], gbf[hl], hl, 0))
        probs.append((qb[hl], kb[hl], vb[hl], gbb[hl], hl, 1))
    outs = _delta_tile(probs, s_scr, cst, half)
    for hl in range(DELTA_HEADS):
        of_ref[hl] = outs[2 * hl]
        ob_ref[hl] = outs[2 * hl + 1]

    if emit_state:
        @pl.when(i == n_tiles - 1)
        def _():
            sout_ref[0] = s_scr[...]


def _delta(q, k, v, gb, s0, *, bsz, n_tiles, emit_state):
    n_tok = bsz * n_tiles * TILE
    hb = DELTA_HEADS
    fwd = lambda b, h, i: (h, b * n_tiles + i, 0)
    bwd = lambda b, h, i: (h, b * n_tiles + n_tiles - 1 - i, 0)
    has_s0 = s0 is not None
    in_specs, args = [], []
    for m in (fwd, bwd):
        in_specs += [pl.BlockSpec((hb, TILE, HEAD_DIM), m)] * 3 + [pl.BlockSpec((hb, TILE, GB_LANES), m)]
        args += [q, k, v, gb]
    state_spec = pl.BlockSpec((1, 2, hb, HEAD_DIM, HEAD_DIM), lambda b, h, i: (b, 0, h, 0, 0))
    if has_s0:
        in_specs.append(state_spec)
        args.append(s0)
    out_specs = [pl.BlockSpec((hb, TILE, HEAD_DIM), fwd), pl.BlockSpec((hb, TILE, HEAD_DIM), bwd)]
    out_shape = [jax.ShapeDtypeStruct((N_HEADS, n_tok, HEAD_DIM), F32)] * 2
    if emit_state:
        out_specs.append(state_spec)
        out_shape.append(jax.ShapeDtypeStruct((bsz, 2, N_HEADS, HEAD_DIM, HEAD_DIM), F32))
    kern = functools.partial(_delta_kernel, has_s0=has_s0, emit_state=emit_state, n_tiles=n_tiles)
    return pl.pallas_call(
        kern,
        grid=(bsz, N_HEADS // hb, n_tiles),
        in_specs=in_specs,
        out_specs=out_specs,
        out_shape=out_shape,
        scratch_shapes=[pltpu.VMEM((2, hb, HEAD_DIM, HEAD_DIM), F32), pltpu.VMEM((N_CAT_CONSTS, CHUNK, TILE), F32),
                        pltpu.VMEM((2, CHUNK, 128), BF16)],
        compiler_params=pltpu.CompilerParams(dimension_semantics=("arbitrary", "arbitrary", "arbitrary"),
                                             vmem_limit_bytes=VMEM_LIMIT),
        name="delta_lat" if has_s0 else "delta_ctx",
    )(*args)


RG_SUB = D_RG // 128


def _rglru_gates(xf_ref, w_ref, br_ref, bi_ref, lam_ref, a_scr, b_scr, d):
    for n in range(N_RG_BLOCKS):
        cols = slice(n * RG_BLOCK, (n + 1) * RG_BLOCK)
        xb = xf_ref[:, cols]
        ri = jnp.dot(xb.astype(BF16), w_ref[d, n], preferred_element_type=F32)
        tr = jnp.tanh(0.5 * ri[:, :RG_BLOCK] + 0.5 * br_ref[d:d + 1, cols])
        ti = jnp.tanh(0.5 * ri[:, RG_BLOCK:] + 0.5 * bi_ref[d:d + 1, cols])
        c2 = (-0.5 * RG_C * LOG2_E) * _softplus(-lam_ref[d:d + 1, cols])
        log2_a = c2 * tr + c2
        a = jnp.exp2(log2_a)
        one_minus_a2 = jnp.tanh((-LN_2) * log2_a) * (a * a + 1.0)
        gain = jnp.where(one_minus_a2 > 0.0, one_minus_a2 * lax.rsqrt(one_minus_a2), 0.0)
        half_x = 0.5 * xb
        a_scr[pl.ds(n, TILE, stride=RG_SUB), :] = a
        b_scr[pl.ds(n, TILE, stride=RG_SUB), :] = gain * (half_x * ti + half_x)


def _rglru_kernel(*refs, has_h0, emit_state, n_tiles):
    if has_h0:
        (xff, xfb, h0_ref), rest = refs[:3], refs[3:]
    else:
        (xff, xfb), rest = refs[:2], refs[2:]
        h0_ref = None
    w_ref, br_ref, bi_ref, lam_ref = rest[:4]
    rest = rest[4:]
    if emit_state:
        hf_ref, hb_ref, sfin_ref, carry, af_scr, bf_scr, ab_scr, bb_scr = rest
    else:
        hf_ref, hb_ref, carry, af_scr, bf_scr, ab_scr, bb_scr = rest
        sfin_ref = None
    i = pl.program_id(1)

    @pl.when(i == 0)
    def _():
        if has_h0:
            carry[...] = h0_ref[0]
        else:
            carry[...] = jnp.zeros_like(carry)

    _rglru_gates(xff, w_ref, br_ref, bi_ref, lam_ref, af_scr, bf_scr, 0)
    _rglru_gates(xfb, w_ref, br_ref, bi_ref, lam_ref, ab_scr, bb_scr, 1)

    def body(g, hs):
        hf, hb = hs
        for r in range(8):
            tf = g * 8 + r
            rows = pl.ds(pl.multiple_of(tf * RG_SUB, RG_SUB), RG_SUB)
            hf = af_scr[rows, :] * hf + bf_scr[rows, :]
            hf_ref[rows, :] = hf
            tb = TILE - 1 - tf
            rows = pl.ds(pl.multiple_of(tb * RG_SUB, RG_SUB), RG_SUB)
            hb = ab_scr[rows, :] * hb + bb_scr[rows, :]
            hb_ref[rows, :] = hb
        return hf, hb

    hf, hb = lax.fori_loop(0, TILE // 8, body, (carry[0], carry[1]))
    carry[0] = hf
    carry[1] = hb

    if emit_state:
        @pl.when(i == n_tiles - 1)
        def _():
            sfin_ref[0] = carry[...]


def _rglru(xf, h0, wcat, br, bi, lam, *, bsz, n_tiles, emit_state):
    n_tok = bsz * n_tiles * TILE
    fwd = lambda b, i: (b * n_tiles + i, 0)
    bwd = lambda b, i: (b * n_tiles + n_tiles - 1 - i, 0)
    state_spec = pl.BlockSpec((1, 2, RG_SUB, 128), lambda b, i: (b, 0, 0, 0))
    has_h0 = h0 is not None
    in_specs = [pl.BlockSpec((TILE, D_RG), fwd), pl.BlockSpec((TILE, D_RG), bwd)]
    args = [xf, xf]
    if has_h0:
        in_specs.append(state_spec)
        args.append(h0.reshape(bsz, 2, RG_SUB, 128))
    in_specs += [_const_spec((2, N_RG_BLOCKS, RG_BLOCK, 2 * RG_BLOCK)), _const_spec((2, D_RG)),
                 _const_spec((2, D_RG)), _const_spec((2, D_RG))]
    args += [wcat, br, bi, lam]
    out_specs = [pl.BlockSpec((TILE * RG_SUB, 128), fwd), pl.BlockSpec((TILE * RG_SUB, 128), bwd)]
    out_shape = [jax.ShapeDtypeStruct((n_tok * RG_SUB, 128), F32)] * 2
    if emit_state:
        out_specs.append(state_spec)
        out_shape.append(jax.ShapeDtypeStruct((bsz, 2, RG_SUB, 128), F32))
    kern = functools.partial(_rglru_kernel, has_h0=has_h0, emit_state=emit_state, n_tiles=n_tiles)
    return pl.pallas_call(
        kern,
        grid=(bsz, n_tiles),
        in_specs=in_specs,
        out_specs=out_specs,
        out_shape=out_shape,
        scratch_shapes=[pltpu.VMEM((2, RG_SUB, 128), F32)] + [pltpu.VMEM((TILE * RG_SUB, 128), F32)] * 4,
        compiler_params=pltpu.CompilerParams(dimension_semantics=("arbitrary", "arbitrary"),
                                             vmem_limit_bytes=VMEM_LIMIT),
        name="rglru_lat" if has_h0 else "rglru_ctx",
    )(*args)


def _out_proj_kernel(x_ref, of_ref, ob_ref, z_ref, hf_ref, hb_ref, gate_ref, mods_ref, dnw_ref, postw_ref,
                     wout_ref, y_ref, mix_scr, tab_scr, *, is_lat, mod_row0, blocks_per_seq):
    i = pl.program_id(0)
    if is_lat:
        @pl.when(i == 0)
        def _():
            tab_scr[...] = _pe_table()

    dnw = dnw_ref[...]
    for h in range(N_HEADS):
        cols = slice(h * HEAD_DIM, (h + 1) * HEAD_DIM)
        o = of_ref[h] + ob_ref[h]
        on = o * lax.rsqrt(jnp.mean(o * o, axis=-1, keepdims=True) + EPS) * dnw
        mix_scr[:, cols] = (on * _silu(z_ref[:, cols])).astype(BF16)
    for n in range(RG_SUB):
        cols = slice(n * 128, (n + 1) * 128)
        rows = pl.ds(n, OUT_ROWS, stride=RG_SUB)
        mix_scr[:, D_DN + n * 128:D_DN + (n + 1) * 128] = (
            (hf_ref[rows, :] + hb_ref[rows, :]) * _silu(gate_ref[:, cols])).astype(BF16)

    y = jnp.dot(mix_scr[...], wout_ref[...], preferred_element_type=F32)
    yn = y * lax.rsqrt(jnp.mean(y * y, axis=-1, keepdims=True) + EPS) * postw_ref[...]
    mod = mods_ref[pl.ds(mod_row0 + (i // blocks_per_seq if is_lat else 0), 1), :]
    xs = x_ref[...]
    if is_lat:
        first_tile = (i % blocks_per_seq) * (OUT_ROWS // TILE)
        xs = xs + jnp.concatenate([_pe_tile(tab_scr, first_tile + j) for j in range(OUT_ROWS // TILE)], axis=0)
    y_ref[...] = xs + mod[:, 2 * D_MODEL:] * yn


def _out_proj(x, o_f, o_b, z, h_f, h_b, gate, mods, dnw, postw, wout, *, is_lat, mod_row0):
    bsz, t, _ = x.shape
    n_tok = bsz * t
    assert n_tok % OUT_ROWS == 0 and (t % OUT_ROWS == 0 or not is_lat)
    tok = lambda i: (i, 0)
    head_tok = lambda i: (0, i, 0)
    kern = functools.partial(_out_proj_kernel, is_lat=is_lat, mod_row0=mod_row0, blocks_per_seq=max(t // OUT_ROWS, 1))
    y = pl.pallas_call(
        kern,
        grid=(n_tok // OUT_ROWS,),
        in_specs=[
            pl.BlockSpec((OUT_ROWS, D_MODEL), tok),
            pl.BlockSpec((N_HEADS, OUT_ROWS, HEAD_DIM), head_tok),
            pl.BlockSpec((N_HEADS, OUT_ROWS, HEAD_DIM), head_tok),
            pl.BlockSpec((OUT_ROWS, D_DN), tok),
            pl.BlockSpec((OUT_ROWS * RG_SUB, 128), tok),
            pl.BlockSpec((OUT_ROWS * RG_SUB, 128), tok),
            pl.BlockSpec((OUT_ROWS, D_RG), tok),
            _const_spec((8, 3 * D_MODEL)),
            _const_spec((1, HEAD_DIM)),
            _const_spec((1, D_MODEL)),
            _const_spec((D_DN + D_RG, D_MODEL)),
        ],
        out_specs=pl.BlockSpec((OUT_ROWS, D_MODEL), tok),
        out_shape=jax.ShapeDtypeStruct((n_tok, D_MODEL), F32),
        scratch_shapes=[pltpu.VMEM((OUT_ROWS, D_DN + D_RG), BF16), pltpu.VMEM((GRID_W, D_MODEL // 2), F32)],
        compiler_params=pltpu.CompilerParams(dimension_semantics=("arbitrary",), vmem_limit_bytes=VMEM_LIMIT),
        name="out_proj_lat" if is_lat else "out_proj_ctx",
    )(x.reshape(n_tok, D_MODEL), o_f, o_b, z, h_f, h_b, gate, mods, dnw, postw, wout)
    return y.reshape(bsz, t, D_MODEL)


def _permute_small(w_cols):
    lead = w_cols.shape[:-1]
    ba = w_cols.reshape(lead + (2, 2, N_HEADS))
    per_head = jnp.moveaxis(ba, -1, -3).reshape(lead + (N_HEADS, 4))
    per_head = jnp.pad(per_head, [(0, 0)] * len(lead) + [(0, 0), (0, GB_LANES - 4)])
    flat = per_head.reshape(lead + (N_HEADS * GB_LANES,))
    return jnp.pad(flat, [(0, 0)] * len(lead) + [(0, 128 - N_HEADS * GB_LANES)])


def _mixer_path(x, mods, weights, s_dn0, s_rg0, *, is_lat, mod_row0):
    (prew, postw, win_p, wout, dncw, alog_row, dtb_row, dnw, rgcw, rgcb, wcat, br, bi, lam) = weights
    bsz, t, _ = x.shape
    n_tiles = t // TILE
    q, k, v, z, xf, gate, gb = _in_proj(x, mods, prew, win_p, dncw, rgcw, rgcb, alog_row, dtb_row,
                                        is_lat=is_lat, mod_row0=mod_row0)
    emit = not is_lat
    dn = _delta(q, k, v, gb, s_dn0, bsz=bsz, n_tiles=n_tiles, emit_state=emit)
    rg = _rglru(xf, s_rg0, wcat, br, bi, lam, bsz=bsz, n_tiles=n_tiles, emit_state=emit)
    y = _out_proj(x, dn[0], dn[1], z, rg[0], rg[1], gate, mods, dnw, postw, wout,
                  is_lat=is_lat, mod_row0=mod_row0)
    if emit:
        return y, dn[2], rg[2].reshape(bsz, 2, D_RG)
    return y


def kernel(x_prompt, x_sample, state_delta, state_rglru, c, c_ctx, ada_w, ada_b, pre_norm_w, post_norm_w, w_in, w_out, dn_conv_w, dn_a_log, dn_dt_bias, dn_norm_w, rg_conv_w, rg_conv_b, rg_w_r, rg_b_r, rg_w_i, rg_b_i, rg_lam):
    l = 0
    n_lat = c.shape[0]
    c_all = jnp.concatenate([c_ctx[None], c, jnp.zeros((8 - 1 - n_lat, D_MODEL), F32)], axis=0)
    mods = _mods(c_all, ada_w[l], ada_b[l][None])

    w = w_in[l].astype(BF16)
    conv_cols = jnp.concatenate([w[:, :OFF_Z], w[:, OFF_RX:OFF_RX + D_RG]], axis=1).reshape(D_MODEL, N_MIX_SEGS, SEG)
    plain_cols = jnp.concatenate([w[:, OFF_Z:OFF_B], w[:, OFF_RX + D_RG:]], axis=1).reshape(D_MODEL, N_MIX_SEGS, PLAIN)
    win_p = jnp.concatenate([jnp.concatenate([conv_cols, plain_cols], axis=2).reshape(D_MODEL, P_SMALL),
                             _permute_small(w[:, OFF_B:OFF_RX])], axis=1)
    zeros16 = jnp.zeros((2 * N_HEADS,), F32)
    alog_row = _permute_small(jnp.concatenate([zeros16, dn_a_log[l].reshape(-1)]))[None]
    dtb_row = _permute_small(jnp.concatenate([zeros16, dn_dt_bias[l].reshape(-1)]))[None]
    wcat = jnp.concatenate([rg_w_r[l], rg_w_i[l]], axis=-1).astype(BF16)
    weights = (pre_norm_w[l][None], post_norm_w[l][None], win_p, w_out[l].astype(BF16), dn_conv_w[l],
               alog_row, dtb_row, dn_norm_w[l][None], rg_conv_w[l], rg_conv_b[l][None], wcat,
               rg_b_r[l], rg_b_i[l], rg_lam[l])

    y_prompt, s_dn, s_rg = _mixer_path(x_prompt, mods, weights, None, None, is_lat=False, mod_row0=0)
    y_sample = _mixer_path(x_sample, mods, weights, state_delta[:, l], state_rglru[:, l], is_lat=True, mod_row0=1)
    return (y_prompt, y_sample, s_dn[:, None].astype(x_prompt.dtype), s_rg[:, None].astype(x_prompt.dtype))
```

```python
import functools
import math

import jax
import jax.numpy as jnp
from jax import lax
from jax.experimental import pallas as pl
from jax.experimental.pallas import tpu as pltpu

F32 = jnp.float32
BF16 = jnp.bfloat16

D_MODEL = 1024
N_HEADS = 8
HEAD_DIM = 128
D_DN = N_HEADS * HEAD_DIM
D_RG = 1024
N_RG_BLOCKS = 8
RG_BLOCK = D_RG // N_RG_BLOCKS
CONV_W = 4
CONV_PAD_L = 2
CHUNK = 64
RG_C = 8.0
EPS = 1e-6
GRID_W = 64
OFF_Z = 3 * D_DN
OFF_B = 4 * D_DN
OFF_A = OFF_B + 2 * N_HEADS
OFF_RX = OFF_A + 2 * N_HEADS
D_IN = OFF_RX + 2 * D_RG

TILE = 256
HALO = 16
N_CHUNKS = TILE // CHUNK
SEG = 512
GB_LANES = 8
DELTA_HEADS = 8
OUT_ROWS = 512
IN_ROWS = 512
OUT_K_SPLIT = 4
PLAIN = 256
MIX = SEG + PLAIN
N_QKV_SEGS = 3 * D_DN // SEG
N_MIX_SEGS = N_QKV_SEGS + D_RG // SEG
assert N_MIX_SEGS * PLAIN == D_DN + D_RG
P_SMALL = N_MIX_SEGS * MIX
D_INP = P_SMALL + 128
NEG_BIG = -1e30
LOG2_E = 1.4426950408889634
LN_2 = 0.6931471805599453
VMEM_LIMIT = 56 * 1024 * 1024


def _sigmoid(x):
    return 0.5 * jnp.tanh(0.5 * x) + 0.5


def _silu(x):
    h = 0.5 * x
    return h * jnp.tanh(h) + h


def _softplus(x):
    return jnp.maximum(x, 0.0) + jnp.log1p(jnp.exp(-jnp.abs(x)))


def _mm(a, b):
    return jnp.dot(a.astype(BF16), b.astype(BF16), preferred_element_type=F32)


def _const_spec(shape):
    nd = len(shape)
    return pl.BlockSpec(shape, lambda *_: (0,) * nd, pipeline_mode=pl.Buffered(1))


def _pe_table():
    quarter = D_MODEL // 4
    j = lax.broadcasted_iota(jnp.int32, (GRID_W, quarter), 0).astype(F32)
    kf = lax.broadcasted_iota(jnp.int32, (GRID_W, quarter), 1).astype(F32)
    freqs = jnp.exp(-math.log(10000.0) * kf / quarter)
    ang = j * freqs
    return jnp.concatenate([jnp.sin(ang), jnp.cos(ang)], axis=1)


def _pe_rows(tab_ref, grid_row, n_rows, col0):
    row_part = jnp.broadcast_to(tab_ref[pl.ds(grid_row, 1), :], (n_rows, D_MODEL // 2))
    col_part = tab_ref[col0:col0 + n_rows, :]
    return jnp.concatenate([row_part, col_part], axis=1)


def _pe_tile(tab_ref, i):
    r0 = i * (TILE // GRID_W)
    return jnp.concatenate([_pe_rows(tab_ref, r0 + j, GRID_W, 0) for j in range(TILE // GRID_W)], axis=0)


def _mods_kernel(c_ref, w_ref, b_ref, o_ref):
    o_ref[...] = _mm(_silu(c_ref[...]), w_ref[...]) + b_ref[...]


def _mods(c_all, ada_w, ada_b):
    return pl.pallas_call(
        _mods_kernel,
        grid=(3,),
        in_specs=[
            pl.BlockSpec((8, D_MODEL), lambda j: (0, 0)),
            pl.BlockSpec((D_MODEL, D_MODEL), lambda j: (0, j)),
            pl.BlockSpec((1, D_MODEL), lambda j: (0, j)),
        ],
        out_specs=pl.BlockSpec((8, D_MODEL), lambda j: (0, j)),
        out_shape=jax.ShapeDtypeStruct((8, 3 * D_MODEL), F32),
        compiler_params=pltpu.CompilerParams(dimension_semantics=("arbitrary",), vmem_limit_bytes=VMEM_LIMIT),
        name="mods",
    )(c_all, ada_w, ada_b)


def _in_proj_kernel(xp_ref, x_ref, xn_ref, mods_ref, prew_ref, win_ref, dncw_ref, rgcw_ref, rgcb_ref,
                    alog_ref, dtb_ref,
                    q_ref, k_ref, v_ref, z_ref, xf_ref, gate_ref, gb_ref, gbt_ref,
                    h_scr, tri_scr, tab_scr, *, is_lat, mod_row0, seq_len):
    i = pl.program_id(0)
    span = min(seq_len, IN_ROWS)
    n_span = IN_ROWS // span
    blocks_per_seq = max(seq_len // IN_ROWS, 1)
    pitch = span + HALO
    m_rows = HALO + n_span * pitch

    @pl.when(i == 0)
    def _():
        r = lax.broadcasted_iota(jnp.int32, (TILE, TILE), 0)
        c = lax.broadcasted_iota(jnp.int32, (TILE, TILE), 1)
        tri_scr[...] = jnp.where(((r // CHUNK) == (c // CHUNK)) & (r >= c), 1.0, 0.0).astype(BF16)
        if is_lat:
            tab_scr[...] = _pe_table()

    mod = mods_ref[pl.ds(mod_row0 + (i // blocks_per_seq if is_lat else 0), 1), :]
    shift = mod[:, :D_MODEL]
    scale = mod[:, D_MODEL:2 * D_MODEL]
    prew = prew_ref[...]

    def norm_mod(xv):
        ms = jnp.mean(xv * xv, axis=-1, keepdims=True)
        return (xv * lax.rsqrt(ms + EPS) * prew) * (1.0 + scale) + shift

    zero_halo = jnp.zeros((HALO, D_MODEL), BF16)
    if is_lat:
        j = i % blocks_per_seq
        r0 = j * (IN_ROWS // GRID_W)
        x_prev = xp_ref[...] + _pe_rows(tab_scr, jnp.maximum(r0 - 1, 0), HALO, GRID_W - HALO)
        x_next = xn_ref[...] + _pe_rows(tab_scr, jnp.minimum(r0 + IN_ROWS // GRID_W, GRID_W - 1), HALO, 0)
        h_scr[0:HALO, :] = jnp.where(j > 0, norm_mod(x_prev), 0.0).astype(BF16)
        h_scr[HALO + span:, :] = jnp.where(j < blocks_per_seq - 1, norm_mod(x_next), 0.0).astype(BF16)
        for t in range(IN_ROWS // TILE):
            xt = x_ref[t * TILE:(t + 1) * TILE, :] + _pe_tile(tab_scr, j * (IN_ROWS // TILE) + t)
            h_scr[HALO + t * TILE:HALO + (t + 1) * TILE, :] = norm_mod(xt).astype(BF16)
    else:
        h_scr[0:HALO, :] = zero_halo
        for sp in range(n_span):
            h_scr[HALO + sp * pitch:HALO + sp * pitch + span, :] = norm_mod(
                x_ref[sp * span:(sp + 1) * span, :]).astype(BF16)
            h_scr[HALO + sp * pitch + span:HALO + (sp + 1) * pitch, :] = zero_halo

    def spans_of(full):
        parts = [full[HALO + sp * pitch:HALO + sp * pitch + span, :] for sp in range(n_span)]
        return parts[0] if n_span == 1 else jnp.concatenate(parts, axis=0)

    def project(col0, width):
        return jnp.dot(h_scr[...], win_ref[:, col0:col0 + width], preferred_element_type=F32)

    def conv(p, cw_ref, cw_col0):
        acc = None
        for j in range(CONV_W):
            shifted = p if j == CONV_PAD_L else pltpu.roll(p, (CONV_PAD_L - j) % m_rows, 0)
            tap = spans_of(shifted) * cw_ref[j:j + 1, cw_col0:cw_col0 + SEG]
            acc = tap if acc is None else acc + tap
        return acc

    def finish_small(_, p):
        raw = spans_of(p)
        beta = _sigmoid(raw)
        g = -jnp.exp(alog_ref[...]) * _softplus(raw + dtb_ref[...])
        g_hi = g.astype(BF16)
        g_r1 = g - g_hi.astype(F32)
        g_mid = g_r1.astype(BF16)
        g_lo = (g_r1 - g_mid.astype(F32)).astype(BF16)
        lower = tri_scr[...]
        cum_f = jnp.concatenate([
            jnp.dot(lower, g_hi[t * TILE:(t + 1) * TILE], preferred_element_type=F32)
            + jnp.dot(lower, g_mid[t * TILE:(t + 1) * TILE], preferred_element_type=F32)
            + jnp.dot(lower, g_lo[t * TILE:(t + 1) * TILE], preferred_element_type=F32)
            for t in range(IN_ROWS // TILE)], axis=0)
        tot = jnp.concatenate([jnp.broadcast_to(cum_f[ci * CHUNK + CHUNK - 1:(ci + 1) * CHUNK, :], (CHUNK, 128))
                               for ci in range(IN_ROWS // CHUNK)], axis=0)
        cum_b = tot - cum_f + g
        lane = lax.broadcasted_iota(jnp.int32, (IN_ROWS, 128), 1) % GB_LANES
        comb = jnp.where(lane < 2, beta, jnp.where(lane == 2, cum_f, jnp.where(lane == 3, cum_b, 0.0)))
        comb_t = comb.T
        for h in range(N_HEADS):
            gb_ref[h] = comb[:, h * GB_LANES:(h + 1) * GB_LANES]
            gbt_ref[h] = comb_t[h * GB_LANES:(h + 1) * GB_LANES, :]

    def finish_mixed(s, p):
        if s < N_QKV_SEGS:
            act = _silu(conv(p[:, :SEG], dncw_ref, s * SEG))
            which = (s * SEG) // D_DN
            out_ref = (q_ref, k_ref, v_ref)[which]
            for hl in range(SEG // HEAD_DIM):
                head = ((s * SEG) % D_DN) // HEAD_DIM + hl
                xh = act[:, hl * HEAD_DIM:(hl + 1) * HEAD_DIM]
                if which < 2:
                    inv = lax.rsqrt(jnp.sum(xh * xh, axis=-1, keepdims=True) + EPS)
                    xh = xh * (inv * HEAD_DIM ** -0.5 if which == 0 else inv)
                out_ref[head] = xh
        else:
            c0 = (s - N_QKV_SEGS) * SEG
            xf_ref[:, c0:c0 + SEG] = conv(p[:, :SEG], rgcw_ref, c0) + rgcb_ref[:, c0:c0 + SEG]
        plain = spans_of(p[:, SEG:])
        if s < D_DN // PLAIN:
            z_ref[:, s * PLAIN:(s + 1) * PLAIN] = plain
        else:
            c0 = (s - D_DN // PLAIN) * PLAIN
            gate_ref[:, c0:c0 + PLAIN] = plain

    segs = [(P_SMALL, 128, finish_small, 0)] + [(s * MIX, MIX, finish_mixed, s) for s in range(N_MIX_SEGS)]
    p_next = project(segs[0][0], segs[0][1])
    for idx, (_, _, finish, s) in enumerate(segs):
        p_cur = p_next
        if idx + 1 < len(segs):
            p_next = project(segs[idx + 1][0], segs[idx + 1][1])
        finish(s, p_cur)


def _in_proj(x, mods, prew, win_p, dncw, rgcw, rgcb, alog_row, dtb_row, *, is_lat, mod_row0):
    bsz, t, _ = x.shape
    n_tok = bsz * t
    assert n_tok % IN_ROWS == 0 and (t % IN_ROWS == 0 if is_lat else IN_ROWS % t == 0)
    span = min(t, IN_ROWS)
    m_rows = HALO + (IN_ROWS // span) * (span + HALO)
    hb = IN_ROWS // HALO
    tok = lambda i: (i, 0)
    head_tok = lambda i: (0, i, 0)
    kern = functools.partial(_in_proj_kernel, is_lat=is_lat, mod_row0=mod_row0, seq_len=t)
    xflat = x.reshape(n_tok, D_MODEL)
    return pl.pallas_call(
        kern,
        grid=(n_tok // IN_ROWS,),
        in_specs=[
            pl.BlockSpec((HALO, D_MODEL), lambda i: (jnp.maximum(i * hb - 1, 0), 0)),
            pl.BlockSpec((IN_ROWS, D_MODEL), tok),
            pl.BlockSpec((HALO, D_MODEL), lambda i: (jnp.minimum((i + 1) * hb, n_tok // HALO - 1), 0)),
            _const_spec((8, 3 * D_MODEL)),
            _const_spec((1, D_MODEL)),
            _const_spec((D_MODEL, D_INP)),
            _const_spec((CONV_W, 3 * D_DN)),
            _const_spec((CONV_W, D_RG)),
            _const_spec((1, D_RG)),
            _const_spec((1, 128)),
            _const_spec((1, 128)),
        ],
        out_specs=[
            pl.BlockSpec((N_HEADS, IN_ROWS, HEAD_DIM), head_tok),
            pl.BlockSpec((N_HEADS, IN_ROWS, HEAD_DIM), head_tok),
            pl.BlockSpec((N_HEADS, IN_ROWS, HEAD_DIM), head_tok),
            pl.BlockSpec((IN_ROWS, D_DN), tok),
            pl.BlockSpec((IN_ROWS, D_RG), tok),
            pl.BlockSpec((IN_ROWS, D_RG), tok),
            pl.BlockSpec((N_HEADS, IN_ROWS, GB_LANES), head_tok),
            pl.BlockSpec((N_HEADS, GB_LANES, IN_ROWS), lambda i: (0, 0, i)),
        ],
        out_shape=[
            jax.ShapeDtypeStruct((N_HEADS, n_tok, HEAD_DIM), F32),
            jax.ShapeDtypeStruct((N_HEADS, n_tok, HEAD_DIM), F32),
            jax.ShapeDtypeStruct((N_HEADS, n_tok, HEAD_DIM), F32),
            jax.ShapeDtypeStruct((n_tok, D_DN), F32),
            jax.ShapeDtypeStruct((n_tok, D_RG), F32),
            jax.ShapeDtypeStruct((n_tok, D_RG), F32),
            jax.ShapeDtypeStruct((N_HEADS, n_tok, GB_LANES), F32),
            jax.ShapeDtypeStruct((N_HEADS, GB_LANES, n_tok), F32),
        ],
        scratch_shapes=[
            pltpu.VMEM((m_rows, D_MODEL), BF16),
            pltpu.VMEM((TILE, TILE), BF16),
            pltpu.VMEM((GRID_W, D_MODEL // 2), F32),
        ],
        compiler_params=pltpu.CompilerParams(dimension_semantics=("arbitrary",), vmem_limit_bytes=VMEM_LIMIT),
        name="in_proj_lat" if is_lat else "in_proj_ctx",
    )(xflat, xflat, xflat, mods, prew, win_p, dncw, rgcw, rgcb, alog_row, dtb_row)


K_EYE, K_INCL_F, K_INCL_B, K_STRICT_F, K_STRICT_B, K_LEVEL0 = 0, 1, 2, 3, 4, 5
N_LEVELS = int(math.log2(CHUNK))
N_CAT_CONSTS = K_LEVEL0 + N_LEVELS


def _delta_consts(cst, half):
    r = lax.broadcasted_iota(jnp.int32, (CHUNK, TILE), 0)
    c = lax.broadcasted_iota(jnp.int32, (CHUNK, TILE), 1) % CHUNK
    cst[K_EYE] = jnp.where(r == c, 1.0, 0.0)
    cst[K_INCL_F] = jnp.where(r >= c, 0.0, NEG_BIG)
    cst[K_INCL_B] = jnp.where(r <= c, 0.0, NEG_BIG)
    cst[K_STRICT_F] = jnp.where(r > c, 0.0, NEG_BIG)
    cst[K_STRICT_B] = jnp.where(r < c, 0.0, NEG_BIG)
    for lv in range(N_LEVELS):
        m = 1 << lv
        cst[K_LEVEL0 + lv] = jnp.where(((r // (2 * m)) == (c // (2 * m))) & ((r // m) != (c // m)), 1.0, 0.0)
    lane = lax.broadcasted_iota(jnp.int32, (CHUNK, 128), 1)
    half[0] = jnp.where(lane < CHUNK, 1.0, 0.0).astype(BF16)
    half[1] = jnp.where(lane >= CHUNK, 1.0, 0.0).astype(BF16)


def _cat_of_diag_blocks(full):
    lo = lax.broadcasted_iota(jnp.int32, (CHUNK, 128), 1) < CHUNK
    tiles = []
    for t in range(TILE // 128):
        a = full[(2 * t) * CHUNK:(2 * t + 1) * CHUNK, t * 128:(t + 1) * 128]
        b = full[(2 * t + 1) * CHUNK:(2 * t + 2) * CHUNK, t * 128:(t + 1) * 128]
        tiles.append(jnp.where(lo, a, b))
    return jnp.concatenate(tiles, axis=1)


def _cat_of_columns(col):
    lo = lax.broadcasted_iota(jnp.int32, (CHUNK, 128), 1) < CHUNK
    tiles = []
    for t in range(TILE // 128):
        a = jnp.broadcast_to(col[(2 * t) * CHUNK:(2 * t + 1) * CHUNK, :], (CHUNK, 128))
        b = jnp.broadcast_to(col[(2 * t + 1) * CHUNK:(2 * t + 2) * CHUNK, :], (CHUNK, 128))
        tiles.append(jnp.where(lo, a, b))
    return jnp.concatenate(tiles, axis=1)


def _block_diag(cat, half):
    zero = jnp.zeros((CHUNK, 128), BF16)
    rows = []
    for j in range(N_CHUNKS):
        t = j // 2
        blk = cat[:, t * 128:(t + 1) * 128] * half[j % 2]
        rows.append(jnp.concatenate([blk if tt == t else zero for tt in range(TILE // 128)], axis=1))
    return jnp.concatenate(rows, axis=0)


def _delta_tile(probs, s_ref, cst, half):
    n = len(probs)
    beta, gcum, qs, kbeta, attn, ncat = [], [], [], [], [], []
    for q, k, v, gb, gbt, hl, d in probs:
        beta.append(gb[:, d:d + 1])
        gcum.append(gb[:, 2 + d:3 + d])
        diff = _cat_of_columns(gcum[-1]) - gbt[2 + d:3 + d, :]
        qs.append(q)
        kbeta.append(k * beta[-1])
        prod = lax.dot_general(jnp.concatenate([qs[-1], kbeta[-1]], axis=0).astype(BF16), k.astype(BF16),
                               (((1,), (1,)), ((), ())), preferred_element_type=F32)
        attn.append(_cat_of_diag_blocks(prod[:TILE]) * jnp.exp(diff + cst[K_INCL_B if d else K_INCL_F]))
        ncat.append(_cat_of_diag_blocks(prod[TILE:]) * jnp.exp(diff + cst[K_STRICT_B if d else K_STRICT_F]))

    xcat = [cst[K_EYE] - ncat[p] * cst[K_LEVEL0] for p in range(n)]
    xbd = [_block_diag(xcat[p].astype(BF16), half) for p in range(n)]
    for lv in range(1, N_LEVELS):
        cbd = [_block_diag((ncat[p] * cst[K_LEVEL0 + lv]).astype(BF16), half) for p in range(n)]
        xc = [jnp.dot(xcat[p].astype(BF16), cbd[p], preferred_element_type=F32) for p in range(n)]
        xcat = [xcat[p] - jnp.dot(xc[p].astype(BF16), xbd[p], preferred_element_type=F32) for p in range(n)]
        xbd = [_block_diag(xcat[p].astype(BF16), half) for p in range(n)]

    u, w, qd, kd, tot = [], [], [], [], []
    for p, (q, k, v, gb, gbt, hl, d) in enumerate(probs):
        eg = jnp.exp(gcum[p])
        rhs = jnp.concatenate([v * beta[p], kbeta[p] * eg], axis=1).astype(BF16)
        sol = jnp.dot(xbd[p], rhs, preferred_element_type=F32)
        u.append(sol[:, :HEAD_DIM])
        w.append(sol[:, HEAD_DIM:])
        qd.append(qs[p] * eg)
        last = [ci * CHUNK if d else ci * CHUNK + CHUNK - 1 for ci in range(N_CHUNKS)]
        tot.append([gcum[p][r0:r0 + 1, :] for r0 in last])
        gtot = jnp.concatenate([jnp.broadcast_to(x, (CHUNK, 1)) for x in tot[p]], axis=0)
        kd.append(k * jnp.exp(gtot - gcum[p]))

    s = [s_ref[d, hl] for (_, _, _, _, _, hl, d) in probs]
    vnew = [[None] * N_CHUNKS for _ in range(n)]
    obase = [[None] * N_CHUNKS for _ in range(n)]
    for step in range(N_CHUNKS):
        for p, (_, _, _, _, _, hl, d) in enumerate(probs):
            ci = N_CHUNKS - 1 - step if d else step
            rows = slice(ci * CHUNK, (ci + 1) * CHUNK)
            ws = _mm(jnp.concatenate([w[p][rows], qd[p][rows]], axis=0), s[p])
            vn = u[p][rows] - ws[:CHUNK]
            obase[p][ci] = ws[CHUNK:]
            vnew[p][ci] = vn.astype(BF16)
            upd = lax.dot_general(kd[p][rows].astype(BF16), vnew[p][ci], (((0,), (0,)), ((), ())),
                                  preferred_element_type=F32)
            s[p] = s[p] * jnp.exp(tot[p][ci]) + upd
    outs = []
    zero = jnp.zeros((CHUNK, HEAD_DIM), BF16)
    for p, (_, _, _, _, _, hl, d) in enumerate(probs):
        s_ref[d, hl] = s[p]
        vbd = jnp.concatenate([jnp.concatenate([vnew[p][j] if jj == j else zero for jj in range(N_CHUNKS)], axis=1)
                               for j in range(N_CHUNKS)], axis=0)
        ocat = jnp.dot(attn[p].astype(BF16), vbd, preferred_element_type=F32)
        outs.append(jnp.concatenate([obase[p][j] + ocat[:, j * HEAD_DIM:(j + 1) * HEAD_DIM]
                                     for j in range(N_CHUNKS)], axis=0))
    return outs


def _delta_kernel(*refs, has_s0, emit_state, n_tiles):
    (qf, kf, vf, gbf, gbtf, qb, kb, vb, gbb, gbtb), rest = refs[:10], refs[10:]
    if has_s0:
        s0_ref, rest = rest[0], rest[1:]
    else:
        s0_ref = None
    if emit_state:
        of_ref, ob_ref, sout_ref, s_scr, cst, half = rest
    else:
        of_ref, ob_ref, s_scr, cst, half = rest
        sout_ref = None
    i = pl.program_id(2)

    @pl.when((pl.program_id(0) == 0) & (pl.program_id(1) == 0) & (i == 0))
    def _():
        _delta_consts(cst, half)

    @pl.when(i == 0)
    def _():
        if has_s0:
            s_scr[...] = s0_ref[0]
        else:
            s_scr[...] = jnp.zeros_like(s_scr)

    probs = []
    for hl in range(DELTA_HEADS):
        probs.append((qf[hl], kf[hl], vf[hl], gbf[hl], gbtf[hl], hl, 0))
        probs.append((qb[hl], kb[hl], vb[hl], gbb[hl], gbtb[hl], hl, 1))
    outs = _delta_tile(probs, s_scr, cst, half)
    for hl in range(DELTA_HEADS):
        of_ref[hl] = outs[2 * hl]
        ob_ref[hl] = outs[2 * hl + 1]

    if emit_state:
        @pl.when(i == n_tiles - 1)
        def _():
            sout_ref[0] = s_scr[...]


def _delta(q, k, v, gb, gbt, s0, *, bsz, n_tiles, emit_state):
    n_tok = bsz * n_tiles * TILE
    hb = DELTA_HEADS
    fwd = lambda b, h, i: (h, b * n_tiles + i, 0)
    bwd = lambda b, h, i: (h, b * n_tiles + n_tiles - 1 - i, 0)
    has_s0 = s0 is not None
    in_specs, args = [], []
    for m in (fwd, bwd):
        in_specs += [pl.BlockSpec((hb, TILE, HEAD_DIM), m)] * 3 + [pl.BlockSpec((hb, TILE, GB_LANES), m)]
        in_specs += [pl.BlockSpec((hb, GB_LANES, TILE), lambda b, h, i, m=m: (m(b, h, i)[0], 0, m(b, h, i)[1]))]
        args += [q, k, v, gb, gbt]
    state_spec = pl.BlockSpec((1, 2, hb, HEAD_DIM, HEAD_DIM), lambda b, h, i: (b, 0, h, 0, 0))
    if has_s0:
        in_specs.append(state_spec)
        args.append(s0)
    out_specs = [pl.BlockSpec((hb, TILE, HEAD_DIM), fwd), pl.BlockSpec((hb, TILE, HEAD_DIM), bwd)]
    out_shape = [jax.ShapeDtypeStruct((N_HEADS, n_tok, HEAD_DIM), F32)] * 2
    if emit_state:
        out_specs.append(state_spec)
        out_shape.append(jax.ShapeDtypeStruct((bsz, 2, N_HEADS, HEAD_DIM, HEAD_DIM), F32))
    kern = functools.partial(_delta_kernel, has_s0=has_s0, emit_state=emit_state, n_tiles=n_tiles)
    return pl.pallas_call(
        kern,
        grid=(bsz, N_HEADS // hb, n_tiles),
        in_specs=in_specs,
        out_specs=out_specs,
        out_shape=out_shape,
        scratch_shapes=[pltpu.VMEM((2, hb, HEAD_DIM, HEAD_DIM), F32), pltpu.VMEM((N_CAT_CONSTS, CHUNK, TILE), F32),
                        pltpu.VMEM((2, CHUNK, 128), BF16)],
        compiler_params=pltpu.CompilerParams(dimension_semantics=("arbitrary", "arbitrary", "arbitrary"),
                                             vmem_limit_bytes=VMEM_LIMIT),
        name="delta_lat" if has_s0 else "delta_ctx",
    )(*args)


RG_SUB = D_RG // 128


def _rglru_gates(xf_ref, w_ref, br_ref, bi_ref, lam_ref, a_scr, b_scr, d):
    for n in range(N_RG_BLOCKS):
        cols = slice(n * RG_BLOCK, (n + 1) * RG_BLOCK)
        xb = xf_ref[:, cols]
        ri = jnp.dot(xb.astype(BF16), w_ref[d, n], preferred_element_type=F32)
        tr = jnp.tanh(0.5 * ri[:, :RG_BLOCK] + 0.5 * br_ref[d:d + 1, cols])
        ti = jnp.tanh(0.5 * ri[:, RG_BLOCK:] + 0.5 * bi_ref[d:d + 1, cols])
        c2 = (-0.5 * RG_C * LOG2_E) * _softplus(-lam_ref[d:d + 1, cols])
        log2_a = c2 * tr + c2
        a = jnp.exp2(log2_a)
        one_minus_a2 = jnp.tanh((-LN_2) * log2_a) * (a * a + 1.0)
        gain = jnp.where(one_minus_a2 > 0.0, one_minus_a2 * lax.rsqrt(one_minus_a2), 0.0)
        half_x = 0.5 * xb
        a_scr[pl.ds(n, TILE, stride=RG_SUB), :] = a
        b_scr[pl.ds(n, TILE, stride=RG_SUB), :] = gain * (half_x * ti + half_x)


def _rglru_kernel(*refs, has_h0, emit_state, n_tiles):
    if has_h0:
        (xff, xfb, h0_ref), rest = refs[:3], refs[3:]
    else:
        (xff, xfb), rest = refs[:2], refs[2:]
        h0_ref = None
    w_ref, br_ref, bi_ref, lam_ref = rest[:4]
    rest = rest[4:]
    if emit_state:
        hf_ref, hb_ref, sfin_ref, carry, af_scr, bf_scr, ab_scr, bb_scr = rest
    else:
        hf_ref, hb_ref, carry, af_scr, bf_scr, ab_scr, bb_scr = rest
        sfin_ref = None
    i = pl.program_id(1)

    @pl.when(i == 0)
    def _():
        if has_h0:
            carry[...] = h0_ref[0]
        else:
            carry[...] = jnp.zeros_like(carry)

    _rglru_gates(xff, w_ref, br_ref, bi_ref, lam_ref, af_scr, bf_scr, 0)
    _rglru_gates(xfb, w_ref, br_ref, bi_ref, lam_ref, ab_scr, bb_scr, 1)

    def body(g, hs):
        hf, hb = hs
        for r in range(8):
            tf = g * 8 + r
            rows = pl.ds(pl.multiple_of(tf * RG_SUB, RG_SUB), RG_SUB)
            hf = af_scr[rows, :] * hf + bf_scr[rows, :]
            hf_ref[rows, :] = hf
            tb = TILE - 1 - tf
            rows = pl.ds(pl.multiple_of(tb * RG_SUB, RG_SUB), RG_SUB)
            hb = ab_scr[rows, :] * hb + bb_scr[rows, :]
            hb_ref[rows, :] = hb
        return hf, hb

    hf, hb = lax.fori_loop(0, TILE // 8, body, (carry[0], carry[1]))
    carry[0] = hf
    carry[1] = hb

    if emit_state:
        @pl.when(i == n_tiles - 1)
        def _():
            sfin_ref[0] = carry[...]


def _rglru(xf, h0, wcat, br, bi, lam, *, bsz, n_tiles, emit_state):
    n_tok = bsz * n_tiles * TILE
    fwd = lambda b, i: (b * n_tiles + i, 0)
    bwd = lambda b, i: (b * n_tiles + n_tiles - 1 - i, 0)
    state_spec = pl.BlockSpec((1, 2, RG_SUB, 128), lambda b, i: (b, 0, 0, 0))
    has_h0 = h0 is not None
    in_specs = [pl.BlockSpec((TILE, D_RG), fwd), pl.BlockSpec((TILE, D_RG), bwd)]
    args = [xf, xf]
    if has_h0:
        in_specs.append(state_spec)
        args.append(h0.reshape(bsz, 2, RG_SUB, 128))
    in_specs += [_const_spec((2, N_RG_BLOCKS, RG_BLOCK, 2 * RG_BLOCK)), _const_spec((2, D_RG)),
                 _const_spec((2, D_RG)), _const_spec((2, D_RG))]
    args += [wcat, br, bi, lam]
    out_specs = [pl.BlockSpec((TILE * RG_SUB, 128), fwd), pl.BlockSpec((TILE * RG_SUB, 128), bwd)]
    out_shape = [jax.ShapeDtypeStruct((n_tok * RG_SUB, 128), F32)] * 2
    if emit_state:
        out_specs.append(state_spec)
        out_shape.append(jax.ShapeDtypeStruct((bsz, 2, RG_SUB, 128), F32))
    kern = functools.partial(_rglru_kernel, has_h0=has_h0, emit_state=emit_state, n_tiles=n_tiles)
    return pl.pallas_call(
        kern,
        grid=(bsz, n_tiles),
        in_specs=in_specs,
        out_specs=out_specs,
        out_shape=out_shape,
        scratch_shapes=[pltpu.VMEM((2, RG_SUB, 128), F32)] + [pltpu.VMEM((TILE * RG_SUB, 128), F32)] * 4,
        compiler_params=pltpu.CompilerParams(dimension_semantics=("arbitrary", "arbitrary"),
                                             vmem_limit_bytes=VMEM_LIMIT),
        name="rglru_lat" if has_h0 else "rglru_ctx",
    )(*args)


def _out_proj_kernel(x_ref, of_ref, ob_ref, z_ref, hf_ref, hb_ref, gate_ref, mods_ref, dnw_ref, postw_ref,
                     wout_ref, y_ref, tab_scr, *, is_lat, mod_row0, blocks_per_seq):
    i = pl.program_id(0)
    if is_lat:
        @pl.when(i == 0)
        def _():
            tab_scr[...] = _pe_table()

    dnw = dnw_ref[...]

    def dn_piece(h):
        cols = slice(h * HEAD_DIM, (h + 1) * HEAD_DIM)
        o = of_ref[h] + ob_ref[h]
        on = o * lax.rsqrt(jnp.mean(o * o, axis=-1, keepdims=True) + EPS) * dnw
        return (on * _silu(z_ref[:, cols])).astype(BF16)

    def rg_piece(n):
        cols = slice(n * 128, (n + 1) * 128)
        rows = pl.ds(n, OUT_ROWS, stride=RG_SUB)
        return ((hf_ref[rows, :] + hb_ref[rows, :]) * _silu(gate_ref[:, cols])).astype(BF16)

    pieces = [functools.partial(dn_piece, h) for h in range(N_HEADS)] + [functools.partial(rg_piece, n)
                                                                          for n in range(RG_SUB)]
    per_block = len(pieces) // OUT_K_SPLIT
    y = None
    for kb in range(OUT_K_SPLIT):
        mix = jnp.concatenate([f() for f in pieces[kb * per_block:(kb + 1) * per_block]], axis=1)
        part = jnp.dot(mix, wout_ref[kb * per_block * 128:(kb + 1) * per_block * 128, :],
                       preferred_element_type=F32)
        y = part if y is None else y + part
    yn = y * lax.rsqrt(jnp.mean(y * y, axis=-1, keepdims=True) + EPS) * postw_ref[...]
    mod = mods_ref[pl.ds(mod_row0 + (i // blocks_per_seq if is_lat else 0), 1), :]
    xs = x_ref[...]
    if is_lat:
        first_tile = (i % blocks_per_seq) * (OUT_ROWS // TILE)
        xs = xs + jnp.concatenate([_pe_tile(tab_scr, first_tile + j) for j in range(OUT_ROWS // TILE)], axis=0)
    y_ref[...] = xs + mod[:, 2 * D_MODEL:] * yn


def _out_proj(x, o_f, o_b, z, h_f, h_b, gate, mods, dnw, postw, wout, *, is_lat, mod_row0):
    bsz, t, _ = x.shape
    n_tok = bsz * t
    assert n_tok % OUT_ROWS == 0 and (t % OUT_ROWS == 0 or not is_lat)
    tok = lambda i: (i, 0)
    head_tok = lambda i: (0, i, 0)
    kern = functools.partial(_out_proj_kernel, is_lat=is_lat, mod_row0=mod_row0, blocks_per_seq=max(t // OUT_ROWS, 1))
    y = pl.pallas_call(
        kern,
        grid=(n_tok // OUT_ROWS,),
        in_specs=[
            pl.BlockSpec((OUT_ROWS, D_MODEL), tok),
            pl.BlockSpec((N_HEADS, OUT_ROWS, HEAD_DIM), head_tok),
            pl.BlockSpec((N_HEADS, OUT_ROWS, HEAD_DIM), head_tok),
            pl.BlockSpec((OUT_ROWS, D_DN), tok),
            pl.BlockSpec((OUT_ROWS * RG_SUB, 128), tok),
            pl.BlockSpec((OUT_ROWS * RG_SUB, 128), tok),
            pl.BlockSpec((OUT_ROWS, D_RG), tok),
            _const_spec((8, 3 * D_MODEL)),
            _const_spec((1, HEAD_DIM)),
            _const_spec((1, D_MODEL)),
            _const_spec((D_DN + D_RG, D_MODEL)),
        ],
        out_specs=pl.BlockSpec((OUT_ROWS, D_MODEL), tok),
        out_shape=jax.ShapeDtypeStruct((n_tok, D_MODEL), F32),
        scratch_shapes=[pltpu.VMEM((GRID_W, D_MODEL // 2), F32)],
        compiler_params=pltpu.CompilerParams(dimension_semantics=("arbitrary",), vmem_limit_bytes=VMEM_LIMIT),
        name="out_proj_lat" if is_lat else "out_proj_ctx",
    )(x.reshape(n_tok, D_MODEL), o_f, o_b, z, h_f, h_b, gate, mods, dnw, postw, wout)
    return y.reshape(bsz, t, D_MODEL)


def _permute_small(w_cols):
    lead = w_cols.shape[:-1]
    ba = w_cols.reshape(lead + (2, 2, N_HEADS))
    per_head = jnp.moveaxis(ba, -1, -3).reshape(lead + (N_HEADS, 4))
    per_head = jnp.pad(per_head, [(0, 0)] * len(lead) + [(0, 0), (0, GB_LANES - 4)])
    flat = per_head.reshape(lead + (N_HEADS * GB_LANES,))
    return jnp.pad(flat, [(0, 0)] * len(lead) + [(0, 128 - N_HEADS * GB_LANES)])


def _mixer_path(x, mods, weights, s_dn0, s_rg0, *, is_lat, mod_row0):
    (prew, postw, win_p, wout, dncw, alog_row, dtb_row, dnw, rgcw, rgcb, wcat, br, bi, lam) = weights
    bsz, t, _ = x.shape
    n_tiles = t // TILE
    q, k, v, z, xf, gate, gb, gbt = _in_proj(x, mods, prew, win_p, dncw, rgcw, rgcb, alog_row, dtb_row,
                                             is_lat=is_lat, mod_row0=mod_row0)
    emit = not is_lat
    dn = _delta(q, k, v, gb, gbt, s_dn0, bsz=bsz, n_tiles=n_tiles, emit_state=emit)
    rg = _rglru(xf, s_rg0, wcat, br, bi, lam, bsz=bsz, n_tiles=n_tiles, emit_state=emit)
    y = _out_proj(x, dn[0], dn[1], z, rg[0], rg[1], gate, mods, dnw, postw, wout,
                  is_lat=is_lat, mod_row0=mod_row0)
    if emit:
        return y, dn[2], rg[2].reshape(bsz, 2, D_RG)
    return y


def kernel(x_prompt, x_sample, state_delta, state_rglru, c, c_ctx, ada_w, ada_b, pre_norm_w, post_norm_w, w_in, w_out, dn_conv_w, dn_a_log, dn_dt_bias, dn_norm_w, rg_conv_w, rg_conv_b, rg_w_r, rg_b_r, rg_w_i, rg_b_i, rg_lam):
    l = 0
    n_lat = c.shape[0]
    c_all = jnp.concatenate([c_ctx[None], c, jnp.zeros((8 - 1 - n_lat, D_MODEL), F32)], axis=0)
    mods = _mods(c_all, ada_w[l], ada_b[l][None])

    w = w_in[l].astype(BF16)
    conv_cols = jnp.concatenate([w[:, :OFF_Z], w[:, OFF_RX:OFF_RX + D_RG]], axis=1).reshape(D_MODEL, N_MIX_SEGS, SEG)
    plain_cols = jnp.concatenate([w[:, OFF_Z:OFF_B], w[:, OFF_RX + D_RG:]], axis=1).reshape(D_MODEL, N_MIX_SEGS, PLAIN)
    win_p = jnp.concatenate([jnp.concatenate([conv_cols, plain_cols], axis=2).reshape(D_MODEL, P_SMALL),
                             _permute_small(w[:, OFF_B:OFF_RX])], axis=1)
    zeros16 = jnp.zeros((2 * N_HEADS,), F32)
    alog_row = _permute_small(jnp.concatenate([zeros16, dn_a_log[l].reshape(-1)]))[None]
    dtb_row = _permute_small(jnp.concatenate([zeros16, dn_dt_bias[l].reshape(-1)]))[None]
    wcat = jnp.concatenate([rg_w_r[l], rg_w_i[l]], axis=-1).astype(BF16)
    weights = (pre_norm_w[l][None], post_norm_w[l][None], win_p, w_out[l].astype(BF16), dn_conv_w[l],
               alog_row, dtb_row, dn_norm_w[l][None], rg_conv_w[l], rg_conv_b[l][None], wcat,
               rg_b_r[l], rg_b_i[l], rg_lam[l])

    y_prompt, s_dn, s_rg = _mixer_path(x_prompt, mods, weights, None, None, is_lat=False, mod_row0=0)
    y_sample = _mixer_path(x_sample, mods, weights, state_delta[:, l], state_rglru[:, l], is_lat=True, mod_row0=1)
    return (y_prompt, y_sample, s_dn[:, None].astype(x_prompt.dtype), s_rg[:, None].astype(x_prompt.dtype))
```

```python
import functools
import math

import jax
import jax.numpy as jnp
from jax import lax
from jax.experimental import pallas as pl
from jax.experimental.pallas import tpu as pltpu

F32 = jnp.float32
BF16 = jnp.bfloat16

D_MODEL = 1024
N_HEADS = 8
HEAD_DIM = 128
D_DN = N_HEADS * HEAD_DIM
D_RG = 1024
N_RG_BLOCKS = 8
RG_BLOCK = D_RG // N_RG_BLOCKS
CONV_W = 4
CONV_PAD_L = 2
CHUNK = 64
RG_C = 8.0
EPS = 1e-6
GRID_W = 64
OFF_Z = 3 * D_DN
OFF_B = 4 * D_DN
OFF_A = OFF_B + 2 * N_HEADS
OFF_RX = OFF_A + 2 * N_HEADS
D_IN = OFF_RX + 2 * D_RG

TILE = 256
HALO = 16
N_CHUNKS = TILE // CHUNK
SEG = 512
GB_LANES = 8
DELTA_HEADS = 8
OUT_ROWS = 512
IN_ROWS = 512
OUT_K_SPLIT = 4
PLAIN = 256
MIX = SEG + PLAIN
N_QKV_SEGS = 3 * D_DN // SEG
N_MIX_SEGS = N_QKV_SEGS + D_RG // SEG
assert N_MIX_SEGS * PLAIN == D_DN + D_RG
P_SMALL = N_MIX_SEGS * MIX
D_INP = P_SMALL + 128
NEG_BIG = -1e30
LOG2_E = 1.4426950408889634
LN_2 = 0.6931471805599453
VMEM_LIMIT = 56 * 1024 * 1024


def _sigmoid(x):
    return 0.5 * jnp.tanh(0.5 * x) + 0.5


def _silu(x):
    h = 0.5 * x
    return h * jnp.tanh(h) + h


def _softplus(x):
    return jnp.maximum(x, 0.0) + jnp.log1p(jnp.exp(-jnp.abs(x)))


def _mm(a, b):
    return jnp.dot(a.astype(BF16), b.astype(BF16), preferred_element_type=F32)


def _const_spec(shape):
    nd = len(shape)
    return pl.BlockSpec(shape, lambda *_: (0,) * nd, pipeline_mode=pl.Buffered(1))


def _pe_table():
    quarter = D_MODEL // 4
    j = lax.broadcasted_iota(jnp.int32, (GRID_W, quarter), 0).astype(F32)
    kf = lax.broadcasted_iota(jnp.int32, (GRID_W, quarter), 1).astype(F32)
    freqs = jnp.exp(-math.log(10000.0) * kf / quarter)
    ang = j * freqs
    return jnp.concatenate([jnp.sin(ang), jnp.cos(ang)], axis=1)


def _pe_rows(tab_ref, grid_row, n_rows, col0):
    row_part = jnp.broadcast_to(tab_ref[pl.ds(grid_row, 1), :], (n_rows, D_MODEL // 2))
    col_part = tab_ref[col0:col0 + n_rows, :]
    return jnp.concatenate([row_part, col_part], axis=1)


def _pe_tile(tab_ref, i):
    r0 = i * (TILE // GRID_W)
    return jnp.concatenate([_pe_rows(tab_ref, r0 + j, GRID_W, 0) for j in range(TILE // GRID_W)], axis=0)


def _mods_kernel(c_ref, w_ref, b_ref, o_ref):
    o_ref[...] = _mm(_silu(c_ref[...]), w_ref[...]) + b_ref[...]


def _mods(c_all, ada_w, ada_b):
    return pl.pallas_call(
        _mods_kernel,
        grid=(3,),
        in_specs=[
            pl.BlockSpec((8, D_MODEL), lambda j: (0, 0)),
            pl.BlockSpec((D_MODEL, D_MODEL), lambda j: (0, j)),
            pl.BlockSpec((1, D_MODEL), lambda j: (0, j)),
        ],
        out_specs=pl.BlockSpec((8, D_MODEL), lambda j: (0, j)),
        out_shape=jax.ShapeDtypeStruct((8, 3 * D_MODEL), F32),
        compiler_params=pltpu.CompilerParams(dimension_semantics=("arbitrary",), vmem_limit_bytes=VMEM_LIMIT),
        name="mods",
    )(c_all, ada_w, ada_b)


def _in_proj_kernel(xp_ref, x_ref, xn_ref, mods_ref, prew_ref, win_ref, dncw_ref, rgcw_ref, rgcb_ref,
                    alog_ref, dtb_ref,
                    q_ref, k_ref, v_ref, z_ref, xf_ref, gate_ref, gb_ref, gbt_ref,
                    h_scr, tri_scr, tab_scr, *, is_lat, mod_row0, seq_len):
    i = pl.program_id(0)
    span = min(seq_len, IN_ROWS)
    n_span = IN_ROWS // span
    blocks_per_seq = max(seq_len // IN_ROWS, 1)
    pitch = span + HALO
    m_rows = HALO + n_span * pitch

    @pl.when(i == 0)
    def _():
        r = lax.broadcasted_iota(jnp.int32, (TILE, TILE), 0)
        c = lax.broadcasted_iota(jnp.int32, (TILE, TILE), 1)
        tri_scr[...] = jnp.where(((r // CHUNK) == (c // CHUNK)) & (r >= c), 1.0, 0.0).astype(BF16)
        if is_lat:
            tab_scr[...] = _pe_table()

    mod = mods_ref[pl.ds(mod_row0 + (i // blocks_per_seq if is_lat else 0), 1), :]
    shift = mod[:, :D_MODEL]
    scale = mod[:, D_MODEL:2 * D_MODEL]
    prew = prew_ref[...]

    def norm_mod(xv):
        ms = jnp.mean(xv * xv, axis=-1, keepdims=True)
        return (xv * lax.rsqrt(ms + EPS) * prew) * (1.0 + scale) + shift

    zero_halo = jnp.zeros((HALO, D_MODEL), BF16)
    if is_lat:
        j = i % blocks_per_seq
        r0 = j * (IN_ROWS // GRID_W)
        x_prev = xp_ref[...] + _pe_rows(tab_scr, jnp.maximum(r0 - 1, 0), HALO, GRID_W - HALO)
        x_next = xn_ref[...] + _pe_rows(tab_scr, jnp.minimum(r0 + IN_ROWS // GRID_W, GRID_W - 1), HALO, 0)
        h_scr[0:HALO, :] = jnp.where(j > 0, norm_mod(x_prev), 0.0).astype(BF16)
        h_scr[HALO + span:, :] = jnp.where(j < blocks_per_seq - 1, norm_mod(x_next), 0.0).astype(BF16)
        for t in range(IN_ROWS // TILE):
            xt = x_ref[t * TILE:(t + 1) * TILE, :] + _pe_tile(tab_scr, j * (IN_ROWS // TILE) + t)
            h_scr[HALO + t * TILE:HALO + (t + 1) * TILE, :] = norm_mod(xt).astype(BF16)
    else:
        h_scr[0:HALO, :] = zero_halo
        for sp in range(n_span):
            h_scr[HALO + sp * pitch:HALO + sp * pitch + span, :] = norm_mod(
                x_ref[sp * span:(sp + 1) * span, :]).astype(BF16)
            h_scr[HALO + sp * pitch + span:HALO + (sp + 1) * pitch, :] = zero_halo

    def spans_of(full):
        parts = [full[HALO + sp * pitch:HALO + sp * pitch + span, :] for sp in range(n_span)]
        return parts[0] if n_span == 1 else jnp.concatenate(parts, axis=0)

    def project(col0, width):
        return jnp.dot(h_scr[...], win_ref[:, col0:col0 + width], preferred_element_type=F32)

    def conv(p, cw_ref, cw_col0):
        acc = None
        for j in range(CONV_W):
            shifted = p if j == CONV_PAD_L else pltpu.roll(p, (CONV_PAD_L - j) % m_rows, 0)
            tap = spans_of(shifted) * cw_ref[j:j + 1, cw_col0:cw_col0 + SEG]
            acc = tap if acc is None else acc + tap
        return acc

    def finish_small(_, p):
        raw = spans_of(p)
        beta = _sigmoid(raw)
        g = -jnp.exp(alog_ref[...]) * _softplus(raw + dtb_ref[...])
        g_hi = g.astype(BF16)
        g_r1 = g - g_hi.astype(F32)
        g_mid = g_r1.astype(BF16)
        g_lo = (g_r1 - g_mid.astype(F32)).astype(BF16)
        lower = tri_scr[...]
        cum_f = jnp.concatenate([
            jnp.dot(lower, g_hi[t * TILE:(t + 1) * TILE], preferred_element_type=F32)
            + jnp.dot(lower, g_mid[t * TILE:(t + 1) * TILE], preferred_element_type=F32)
            + jnp.dot(lower, g_lo[t * TILE:(t + 1) * TILE], preferred_element_type=F32)
            for t in range(IN_ROWS // TILE)], axis=0)
        tot = jnp.concatenate([jnp.broadcast_to(cum_f[ci * CHUNK + CHUNK - 1:(ci + 1) * CHUNK, :], (CHUNK, 128))
                               for ci in range(IN_ROWS // CHUNK)], axis=0)
        cum_b = tot - cum_f + g
        lane = lax.broadcasted_iota(jnp.int32, (IN_ROWS, 128), 1) % GB_LANES
        comb = jnp.where(lane < 2, beta, jnp.where(lane == 2, cum_f, jnp.where(lane == 3, cum_b, 0.0)))
        comb_t = comb.T
        for h in range(N_HEADS):
            gb_ref[h] = comb[:, h * GB_LANES:(h + 1) * GB_LANES]
            gbt_ref[h] = comb_t[h * GB_LANES:(h + 1) * GB_LANES, :]

    def finish_mixed(s, p):
        if s < N_QKV_SEGS:
            act = _silu(conv(p[:, :SEG], dncw_ref, s * SEG))
            which = (s * SEG) // D_DN
            out_ref = (q_ref, k_ref, v_ref)[which]
            for hl in range(SEG // HEAD_DIM):
                head = ((s * SEG) % D_DN) // HEAD_DIM + hl
                xh = act[:, hl * HEAD_DIM:(hl + 1) * HEAD_DIM]
                if which < 2:
                    inv = lax.rsqrt(jnp.sum(xh * xh, axis=-1, keepdims=True) + EPS)
                    xh = xh * (inv * HEAD_DIM ** -0.5 if which == 0 else inv)
                out_ref[head] = xh
        else:
            c0 = (s - N_QKV_SEGS) * SEG
            xf_ref[:, c0:c0 + SEG] = conv(p[:, :SEG], rgcw_ref, c0) + rgcb_ref[:, c0:c0 + SEG]
        plain = spans_of(p[:, SEG:]).astype(BF16)
        if s < D_DN // PLAIN:
            z_ref[:, s * PLAIN:(s + 1) * PLAIN] = plain
        else:
            c0 = (s - D_DN // PLAIN) * PLAIN
            gate_ref[:, c0:c0 + PLAIN] = plain

    segs = [(P_SMALL, 128, finish_small, 0)] + [(s * MIX, MIX, finish_mixed, s) for s in range(N_MIX_SEGS)]
    p_next = project(segs[0][0], segs[0][1])
    for idx, (_, _, finish, s) in enumerate(segs):
        p_cur = p_next
        if idx + 1 < len(segs):
            p_next = project(segs[idx + 1][0], segs[idx + 1][1])
        finish(s, p_cur)


def _in_proj(x, mods, prew, win_p, dncw, rgcw, rgcb, alog_row, dtb_row, *, is_lat, mod_row0):
    bsz, t, _ = x.shape
    n_tok = bsz * t
    assert n_tok % IN_ROWS == 0 and (t % IN_ROWS == 0 if is_lat else IN_ROWS % t == 0)
    span = min(t, IN_ROWS)
    m_rows = HALO + (IN_ROWS // span) * (span + HALO)
    hb = IN_ROWS // HALO
    tok = lambda i: (i, 0)
    head_tok = lambda i: (0, i, 0)
    kern = functools.partial(_in_proj_kernel, is_lat=is_lat, mod_row0=mod_row0, seq_len=t)
    xflat = x.reshape(n_tok, D_MODEL)
    return pl.pallas_call(
        kern,
        grid=(n_tok // IN_ROWS,),
        in_specs=[
            pl.BlockSpec((HALO, D_MODEL), lambda i: (jnp.maximum(i * hb - 1, 0), 0)),
            pl.BlockSpec((IN_ROWS, D_MODEL), tok),
            pl.BlockSpec((HALO, D_MODEL), lambda i: (jnp.minimum((i + 1) * hb, n_tok // HALO - 1), 0)),
            _const_spec((8, 3 * D_MODEL)),
            _const_spec((1, D_MODEL)),
            _const_spec((D_MODEL, D_INP)),
            _const_spec((CONV_W, 3 * D_DN)),
            _const_spec((CONV_W, D_RG)),
            _const_spec((1, D_RG)),
            _const_spec((1, 128)),
            _const_spec((1, 128)),
        ],
        out_specs=[
            pl.BlockSpec((N_HEADS, IN_ROWS, HEAD_DIM), head_tok),
            pl.BlockSpec((N_HEADS, IN_ROWS, HEAD_DIM), head_tok),
            pl.BlockSpec((N_HEADS, IN_ROWS, HEAD_DIM), head_tok),
            pl.BlockSpec((IN_ROWS, D_DN), tok),
            pl.BlockSpec((IN_ROWS, D_RG), tok),
            pl.BlockSpec((IN_ROWS, D_RG), tok),
            pl.BlockSpec((N_HEADS, IN_ROWS, GB_LANES), head_tok),
            pl.BlockSpec((N_HEADS, GB_LANES, IN_ROWS), lambda i: (0, 0, i)),
        ],
        out_shape=[
            jax.ShapeDtypeStruct((N_HEADS, n_tok, HEAD_DIM), F32),
            jax.ShapeDtypeStruct((N_HEADS, n_tok, HEAD_DIM), F32),
            jax.ShapeDtypeStruct((N_HEADS, n_tok, HEAD_DIM), F32),
            jax.ShapeDtypeStruct((n_tok, D_DN), BF16),
            jax.ShapeDtypeStruct((n_tok, D_RG), F32),
            jax.ShapeDtypeStruct((n_tok, D_RG), BF16),
            jax.ShapeDtypeStruct((N_HEADS, n_tok, GB_LANES), F32),
            jax.ShapeDtypeStruct((N_HEADS, GB_LANES, n_tok), F32),
        ],
        scratch_shapes=[
            pltpu.VMEM((m_rows, D_MODEL), BF16),
            pltpu.VMEM((TILE, TILE), BF16),
            pltpu.VMEM((GRID_W, D_MODEL // 2), F32),
        ],
        compiler_params=pltpu.CompilerParams(dimension_semantics=("arbitrary",), vmem_limit_bytes=VMEM_LIMIT),
        name="in_proj_lat" if is_lat else "in_proj_ctx",
    )(xflat, xflat, xflat, mods, prew, win_p, dncw, rgcw, rgcb, alog_row, dtb_row)


K_EYE, K_INCL_F, K_INCL_B, K_STRICT_F, K_STRICT_B, K_LEVEL0 = 0, 1, 2, 3, 4, 5
N_LEVELS = int(math.log2(CHUNK))
N_CAT_CONSTS = K_LEVEL0 + N_LEVELS


def _delta_consts(cst, half):
    r = lax.broadcasted_iota(jnp.int32, (CHUNK, TILE), 0)
    c = lax.broadcasted_iota(jnp.int32, (CHUNK, TILE), 1) % CHUNK
    cst[K_EYE] = jnp.where(r == c, 1.0, 0.0)
    cst[K_INCL_F] = jnp.where(r >= c, 0.0, NEG_BIG)
    cst[K_INCL_B] = jnp.where(r <= c, 0.0, NEG_BIG)
    cst[K_STRICT_F] = jnp.where(r > c, 0.0, NEG_BIG)
    cst[K_STRICT_B] = jnp.where(r < c, 0.0, NEG_BIG)
    for lv in range(N_LEVELS):
        m = 1 << lv
        cst[K_LEVEL0 + lv] = jnp.where(((r // (2 * m)) == (c // (2 * m))) & ((r // m) != (c // m)), 1.0, 0.0)
    lane = lax.broadcasted_iota(jnp.int32, (CHUNK, 128), 1)
    half[0] = jnp.where(lane < CHUNK, 1.0, 0.0).astype(BF16)
    half[1] = jnp.where(lane >= CHUNK, 1.0, 0.0).astype(BF16)


def _cat_of_diag_blocks(full):
    lo = lax.broadcasted_iota(jnp.int32, (CHUNK, 128), 1) < CHUNK
    tiles = []
    for t in range(TILE // 128):
        a = full[(2 * t) * CHUNK:(2 * t + 1) * CHUNK, t * 128:(t + 1) * 128]
        b = full[(2 * t + 1) * CHUNK:(2 * t + 2) * CHUNK, t * 128:(t + 1) * 128]
        tiles.append(jnp.where(lo, a, b))
    return jnp.concatenate(tiles, axis=1)


def _cat_of_columns(col):
    lo = lax.broadcasted_iota(jnp.int32, (CHUNK, 128), 1) < CHUNK
    tiles = []
    for t in range(TILE // 128):
        a = jnp.broadcast_to(col[(2 * t) * CHUNK:(2 * t + 1) * CHUNK, :], (CHUNK, 128))
        b = jnp.broadcast_to(col[(2 * t + 1) * CHUNK:(2 * t + 2) * CHUNK, :], (CHUNK, 128))
        tiles.append(jnp.where(lo, a, b))
    return jnp.concatenate(tiles, axis=1)


def _block_diag(cat, half):
    zero = jnp.zeros((CHUNK, 128), BF16)
    rows = []
    for j in range(N_CHUNKS):
        t = j // 2
        blk = cat[:, t * 128:(t + 1) * 128] * half[j % 2]
        rows.append(jnp.concatenate([blk if tt == t else zero for tt in range(TILE // 128)], axis=1))
    return jnp.concatenate(rows, axis=0)


def _delta_tile(probs, s_ref, cst, half):
    n = len(probs)
    beta, gcum, qs, kbeta, attn, ncat = [], [], [], [], [], []
    for q, k, v, gb, gbt, hl, d in probs:
        beta.append(gb[:, d:d + 1])
        gcum.append(gb[:, 2 + d:3 + d])
        diff = _cat_of_columns(gcum[-1]) - gbt[2 + d:3 + d, :]
        qs.append(q)
        kbeta.append(k * beta[-1])
        prod = lax.dot_general(jnp.concatenate([qs[-1], kbeta[-1]], axis=0).astype(BF16), k.astype(BF16),
                               (((1,), (1,)), ((), ())), preferred_element_type=F32)
        attn.append(_cat_of_diag_blocks(prod[:TILE]) * jnp.exp(diff + cst[K_INCL_B if d else K_INCL_F]))
        ncat.append(_cat_of_diag_blocks(prod[TILE:]) * jnp.exp(diff + cst[K_STRICT_B if d else K_STRICT_F]))

    xcat = [cst[K_EYE] - ncat[p] * cst[K_LEVEL0] for p in range(n)]
    xbd = [_block_diag(xcat[p].astype(BF16), half) for p in range(n)]
    for lv in range(1, N_LEVELS):
        cbd = [_block_diag((ncat[p] * cst[K_LEVEL0 + lv]).astype(BF16), half) for p in range(n)]
        xc = [jnp.dot(xcat[p].astype(BF16), cbd[p], preferred_element_type=F32) for p in range(n)]
        xcat = [xcat[p] - jnp.dot(xc[p].astype(BF16), xbd[p], preferred_element_type=F32) for p in range(n)]
        xbd = [_block_diag(xcat[p].astype(BF16), half) for p in range(n)]

    u, w, qd, kd, tot = [], [], [], [], []
    for p, (q, k, v, gb, gbt, hl, d) in enumerate(probs):
        eg = jnp.exp(gcum[p])
        rhs = jnp.concatenate([v * beta[p], kbeta[p] * eg], axis=1).astype(BF16)
        sol = jnp.dot(xbd[p], rhs, preferred_element_type=F32)
        u.append(sol[:, :HEAD_DIM])
        w.append(sol[:, HEAD_DIM:])
        qd.append(qs[p] * eg)
        last = [ci * CHUNK if d else ci * CHUNK + CHUNK - 1 for ci in range(N_CHUNKS)]
        tot.append([gcum[p][r0:r0 + 1, :] for r0 in last])
        gtot = jnp.concatenate([jnp.broadcast_to(x, (CHUNK, 1)) for x in tot[p]], axis=0)
        kd.append(k * jnp.exp(gtot - gcum[p]))

    s = [s_ref[d, hl] for (_, _, _, _, _, hl, d) in probs]
    vnew = [[None] * N_CHUNKS for _ in range(n)]
    obase = [[None] * N_CHUNKS for _ in range(n)]
    for step in range(N_CHUNKS):
        for p, (_, _, _, _, _, hl, d) in enumerate(probs):
            ci = N_CHUNKS - 1 - step if d else step
            rows = slice(ci * CHUNK, (ci + 1) * CHUNK)
            ws = _mm(jnp.concatenate([w[p][rows], qd[p][rows]], axis=0), s[p])
            vn = u[p][rows] - ws[:CHUNK]
            obase[p][ci] = ws[CHUNK:]
            vnew[p][ci] = vn.astype(BF16)
            upd = lax.dot_general(kd[p][rows].astype(BF16), vnew[p][ci], (((0,), (0,)), ((), ())),
                                  preferred_element_type=F32)
            s[p] = s[p] * jnp.exp(tot[p][ci]) + upd
    outs = []
    zero = jnp.zeros((CHUNK, HEAD_DIM), BF16)
    for p, (_, _, _, _, _, hl, d) in enumerate(probs):
        s_ref[d, hl] = s[p]
        vbd = jnp.concatenate([jnp.concatenate([vnew[p][j] if jj == j else zero for jj in range(N_CHUNKS)], axis=1)
                               for j in range(N_CHUNKS)], axis=0)
        ocat = jnp.dot(attn[p].astype(BF16), vbd, preferred_element_type=F32)
        outs.append(jnp.concatenate([obase[p][j] + ocat[:, j * HEAD_DIM:(j + 1) * HEAD_DIM]
                                     for j in range(N_CHUNKS)], axis=0))
    return outs


def _delta_kernel(*refs, has_s0, emit_state, n_tiles):
    (qf, kf, vf, gbf, gbtf, qb, kb, vb, gbb, gbtb), rest = refs[:10], refs[10:]
    if has_s0:
        s0_ref, rest = rest[0], rest[1:]
    else:
        s0_ref = None
    if emit_state:
        of_ref, ob_ref, sout_ref, s_scr, cst, half = rest
    else:
        of_ref, ob_ref, s_scr, cst, half = rest
        sout_ref = None
    i = pl.program_id(2)

    @pl.when((pl.program_id(0) == 0) & (pl.program_id(1) == 0) & (i == 0))
    def _():
        _delta_consts(cst, half)

    @pl.when(i == 0)
    def _():
        if has_s0:
            s_scr[...] = s0_ref[0]
        else:
            s_scr[...] = jnp.zeros_like(s_scr)

    probs = []
    for hl in range(DELTA_HEADS):
        probs.append((qf[hl], kf[hl], vf[hl], gbf[hl], gbtf[hl], hl, 0))
        probs.append((qb[hl], kb[hl], vb[hl], gbb[hl], gbtb[hl], hl, 1))
    outs = _delta_tile(probs, s_scr, cst, half)
    for hl in range(DELTA_HEADS):
        of_ref[hl] = outs[2 * hl].astype(BF16)
        ob_ref[hl] = outs[2 * hl + 1].astype(BF16)

    if emit_state:
        @pl.when(i == n_tiles - 1)
        def _():
            sout_ref[0] = s_scr[...]


def _delta(q, k, v, gb, gbt, s0, *, bsz, n_tiles, emit_state):
    n_tok = bsz * n_tiles * TILE
    hb = DELTA_HEADS
    fwd = lambda b, h, i: (h, b * n_tiles + i, 0)
    bwd = lambda b, h, i: (h, b * n_tiles + n_tiles - 1 - i, 0)
    has_s0 = s0 is not None
    in_specs, args = [], []
    for m in (fwd, bwd):
        in_specs += [pl.BlockSpec((hb, TILE, HEAD_DIM), m)] * 3 + [pl.BlockSpec((hb, TILE, GB_LANES), m)]
        in_specs += [pl.BlockSpec((hb, GB_LANES, TILE), lambda b, h, i, m=m: (m(b, h, i)[0], 0, m(b, h, i)[1]))]
        args += [q, k, v, gb, gbt]
    state_spec = pl.BlockSpec((1, 2, hb, HEAD_DIM, HEAD_DIM), lambda b, h, i: (b, 0, h, 0, 0))
    if has_s0:
        in_specs.append(state_spec)
        args.append(s0)
    out_specs = [pl.BlockSpec((hb, TILE, HEAD_DIM), fwd), pl.BlockSpec((hb, TILE, HEAD_DIM), bwd)]
    out_shape = [jax.ShapeDtypeStruct((N_HEADS, n_tok, HEAD_DIM), BF16)] * 2
    if emit_state:
        out_specs.append(state_spec)
        out_shape.append(jax.ShapeDtypeStruct((bsz, 2, N_HEADS, HEAD_DIM, HEAD_DIM), F32))
    kern = functools.partial(_delta_kernel, has_s0=has_s0, emit_state=emit_state, n_tiles=n_tiles)
    return pl.pallas_call(
        kern,
        grid=(bsz, N_HEADS // hb, n_tiles),
        in_specs=in_specs,
        out_specs=out_specs,
        out_shape=out_shape,
        scratch_shapes=[pltpu.VMEM((2, hb, HEAD_DIM, HEAD_DIM), F32), pltpu.VMEM((N_CAT_CONSTS, CHUNK, TILE), F32),
                        pltpu.VMEM((2, CHUNK, 128), BF16)],
        compiler_params=pltpu.CompilerParams(dimension_semantics=("arbitrary", "arbitrary", "arbitrary"),
                                             vmem_limit_bytes=VMEM_LIMIT),
        name="delta_lat" if has_s0 else "delta_ctx",
    )(*args)


RG_SUB = D_RG // 128
RG_PARTS = 2


def _rglru_gates(xf_ref, w_ref, br_ref, bi_ref, lam_ref, a_scr, b_scr, d, row0, n_rows):
    for n in range(N_RG_BLOCKS):
        cols = slice(n * RG_BLOCK, (n + 1) * RG_BLOCK)
        xb = xf_ref[row0:row0 + n_rows, cols]
        ri = jnp.dot(xb.astype(BF16), w_ref[d, n], preferred_element_type=F32)
        tr = jnp.tanh(0.5 * ri[:, :RG_BLOCK] + 0.5 * br_ref[d:d + 1, cols])
        ti = jnp.tanh(0.5 * ri[:, RG_BLOCK:] + 0.5 * bi_ref[d:d + 1, cols])
        c2 = (-0.5 * RG_C * LOG2_E) * _softplus(-lam_ref[d:d + 1, cols])
        log2_a = c2 * tr + c2
        a = jnp.exp2(log2_a)
        one_minus_a2 = jnp.tanh((-LN_2) * log2_a) * (a * a + 1.0)
        gain = jnp.where(one_minus_a2 > 0.0, one_minus_a2 * lax.rsqrt(one_minus_a2), 0.0)
        half_x = 0.5 * xb
        rows = pl.ds(row0 * RG_SUB + n, n_rows, stride=RG_SUB)
        a_scr[rows, :] = a
        b_scr[rows, :] = gain * (half_x * ti + half_x)


def _rglru_kernel(*refs, has_h0, emit_state, n_tiles):
    if has_h0:
        (xff, xfb, h0_ref), rest = refs[:3], refs[3:]
    else:
        (xff, xfb), rest = refs[:2], refs[2:]
        h0_ref = None
    w_ref, br_ref, bi_ref, lam_ref = rest[:4]
    rest = rest[4:]
    if emit_state:
        hf_ref, hb_ref, sfin_ref, carry, af_scr, bf_scr, ab_scr, bb_scr = rest
    else:
        hf_ref, hb_ref, carry, af_scr, bf_scr, ab_scr, bb_scr = rest
        sfin_ref = None
    i = pl.program_id(1)

    @pl.when(i == 0)
    def _():
        if has_h0:
            carry[...] = h0_ref[0]
        else:
            carry[...] = jnp.zeros_like(carry)

    part = TILE // RG_PARTS
    hf = carry[0]
    hb = carry[1]
    for pi in range(RG_PARTS):
        f0 = pi * part
        b0 = TILE - (pi + 1) * part
        _rglru_gates(xff, w_ref, br_ref, bi_ref, lam_ref, af_scr, bf_scr, 0, f0, part)
        _rglru_gates(xfb, w_ref, br_ref, bi_ref, lam_ref, ab_scr, bb_scr, 1, b0, part)
        for r in range(part):
            rows = pl.ds((f0 + r) * RG_SUB, RG_SUB)
            hf = af_scr[rows, :] * hf + bf_scr[rows, :]
            hf_ref[rows, :] = hf
            rows = pl.ds((b0 + part - 1 - r) * RG_SUB, RG_SUB)
            hb = ab_scr[rows, :] * hb + bb_scr[rows, :]
            hb_ref[rows, :] = hb
    carry[0] = hf
    carry[1] = hb

    if emit_state:
        @pl.when(i == n_tiles - 1)
        def _():
            sfin_ref[0] = carry[...]


def _rglru(xf, h0, wcat, br, bi, lam, *, bsz, n_tiles, emit_state):
    n_tok = bsz * n_tiles * TILE
    fwd = lambda b, i: (b * n_tiles + i, 0)
    bwd = lambda b, i: (b * n_tiles + n_tiles - 1 - i, 0)
    state_spec = pl.BlockSpec((1, 2, RG_SUB, 128), lambda b, i: (b, 0, 0, 0))
    has_h0 = h0 is not None
    in_specs = [pl.BlockSpec((TILE, D_RG), fwd), pl.BlockSpec((TILE, D_RG), bwd)]
    args = [xf, xf]
    if has_h0:
        in_specs.append(state_spec)
        args.append(h0.reshape(bsz, 2, RG_SUB, 128))
    in_specs += [_const_spec((2, N_RG_BLOCKS, RG_BLOCK, 2 * RG_BLOCK)), _const_spec((2, D_RG)),
                 _const_spec((2, D_RG)), _const_spec((2, D_RG))]
    args += [wcat, br, bi, lam]
    out_specs = [pl.BlockSpec((TILE * RG_SUB, 128), fwd), pl.BlockSpec((TILE * RG_SUB, 128), bwd)]
    out_shape = [jax.ShapeDtypeStruct((n_tok * RG_SUB, 128), F32)] * 2
    if emit_state:
        out_specs.append(state_spec)
        out_shape.append(jax.ShapeDtypeStruct((bsz, 2, RG_SUB, 128), F32))
    kern = functools.partial(_rglru_kernel, has_h0=has_h0, emit_state=emit_state, n_tiles=n_tiles)
    return pl.pallas_call(
        kern,
        grid=(bsz, n_tiles),
        in_specs=in_specs,
        out_specs=out_specs,
        out_shape=out_shape,
        scratch_shapes=[pltpu.VMEM((2, RG_SUB, 128), F32)] + [pltpu.VMEM((TILE * RG_SUB, 128), F32)] * 4,
        compiler_params=pltpu.CompilerParams(dimension_semantics=("arbitrary", "arbitrary"),
                                             vmem_limit_bytes=VMEM_LIMIT),
        name="rglru_lat" if has_h0 else "rglru_ctx",
    )(*args)


def _out_proj_kernel(x_ref, of_ref, ob_ref, z_ref, hf_ref, hb_ref, gate_ref, mods_ref, dnw_ref, postw_ref,
                     wout_ref, y_ref, tab_scr, *, is_lat, mod_row0, blocks_per_seq):
    i = pl.program_id(0)
    if is_lat:
        @pl.when(i == 0)
        def _():
            tab_scr[...] = _pe_table()

    dnw = dnw_ref[...]

    def dn_piece(h):
        cols = slice(h * HEAD_DIM, (h + 1) * HEAD_DIM)
        o = of_ref[h].astype(F32) + ob_ref[h].astype(F32)
        on = o * lax.rsqrt(jnp.mean(o * o, axis=-1, keepdims=True) + EPS) * dnw
        return (on * _silu(z_ref[:, cols].astype(F32))).astype(BF16)

    def rg_piece(n):
        cols = slice(n * 128, (n + 1) * 128)
        rows = pl.ds(n, OUT_ROWS, stride=RG_SUB)
        return ((hf_ref[rows, :] + hb_ref[rows, :]) * _silu(gate_ref[:, cols].astype(F32))).astype(BF16)

    pieces = [functools.partial(dn_piece, h) for h in range(N_HEADS)] + [functools.partial(rg_piece, n)
                                                                          for n in range(RG_SUB)]
    per_block = len(pieces) // OUT_K_SPLIT
    y = None
    for kb in range(OUT_K_SPLIT):
        mix = jnp.concatenate([f() for f in pieces[kb * per_block:(kb + 1) * per_block]], axis=1)
        part = jnp.dot(mix, wout_ref[kb * per_block * 128:(kb + 1) * per_block * 128, :],
                       preferred_element_type=F32)
        y = part if y is None else y + part
    yn = y * lax.rsqrt(jnp.mean(y * y, axis=-1, keepdims=True) + EPS) * postw_ref[...]
    mod = mods_ref[pl.ds(mod_row0 + (i // blocks_per_seq if is_lat else 0), 1), :]
    xs = x_ref[...]
    if is_lat:
        first_tile = (i % blocks_per_seq) * (OUT_ROWS // TILE)
        xs = xs + jnp.concatenate([_pe_tile(tab_scr, first_tile + j) for j in range(OUT_ROWS // TILE)], axis=0)
    y_ref[...] = xs + mod[:, 2 * D_MODEL:] * yn


def _out_proj(x, o_f, o_b, z, h_f, h_b, gate, mods, dnw, postw, wout, *, is_lat, mod_row0):
    bsz, t, _ = x.shape
    n_tok = bsz * t
    assert n_tok % OUT_ROWS == 0 and (t % OUT_ROWS == 0 or not is_lat)
    tok = lambda i: (i, 0)
    head_tok = lambda i: (0, i, 0)
    kern = functools.partial(_out_proj_kernel, is_lat=is_lat, mod_row0=mod_row0, blocks_per_seq=max(t // OUT_ROWS, 1))
    y = pl.pallas_call(
        kern,
        grid=(n_tok // OUT_ROWS,),
        in_specs=[
            pl.BlockSpec((OUT_ROWS, D_MODEL), tok),
            pl.BlockSpec((N_HEADS, OUT_ROWS, HEAD_DIM), head_tok),
            pl.BlockSpec((N_HEADS, OUT_ROWS, HEAD_DIM), head_tok),
            pl.BlockSpec((OUT_ROWS, D_DN), tok),
            pl.BlockSpec((OUT_ROWS * RG_SUB, 128), tok),
            pl.BlockSpec((OUT_ROWS * RG_SUB, 128), tok),
            pl.BlockSpec((OUT_ROWS, D_RG), tok),
            _const_spec((8, 3 * D_MODEL)),
            _const_spec((1, HEAD_DIM)),
            _const_spec((1, D_MODEL)),
            _const_spec((D_DN + D_RG, D_MODEL)),
        ],
        out_specs=pl.BlockSpec((OUT_ROWS, D_MODEL), tok),
        out_shape=jax.ShapeDtypeStruct((n_tok, D_MODEL), F32),
        scratch_shapes=[pltpu.VMEM((GRID_W, D_MODEL // 2), F32)],
        compiler_params=pltpu.CompilerParams(dimension_semantics=("arbitrary",), vmem_limit_bytes=VMEM_LIMIT),
        name="out_proj_lat" if is_lat else "out_proj_ctx",
    )(x.reshape(n_tok, D_MODEL), o_f, o_b, z, h_f, h_b, gate, mods, dnw, postw, wout)
    return y.reshape(bsz, t, D_MODEL)


def _permute_small(w_cols):
    lead = w_cols.shape[:-1]
    ba = w_cols.reshape(lead + (2, 2, N_HEADS))
    per_head = jnp.moveaxis(ba, -1, -3).reshape(lead + (N_HEADS, 4))
    per_head = jnp.pad(per_head, [(0, 0)] * len(lead) + [(0, 0), (0, GB_LANES - 4)])
    flat = per_head.reshape(lead + (N_HEADS * GB_LANES,))
    return jnp.pad(flat, [(0, 0)] * len(lead) + [(0, 128 - N_HEADS * GB_LANES)])


def _mixer_path(x, mods, weights, s_dn0, s_rg0, *, is_lat, mod_row0):
    (prew, postw, win_p, wout, dncw, alog_row, dtb_row, dnw, rgcw, rgcb, wcat, br, bi, lam) = weights
    bsz, t, _ = x.shape
    n_tiles = t // TILE
    q, k, v, z, xf, gate, gb, gbt = _in_proj(x, mods, prew, win_p, dncw, rgcw, rgcb, alog_row, dtb_row,
                                             is_lat=is_lat, mod_row0=mod_row0)
    emit = not is_lat
    dn = _delta(q, k, v, gb, gbt, s_dn0, bsz=bsz, n_tiles=n_tiles, emit_state=emit)
    rg = _rglru(xf, s_rg0, wcat, br, bi, lam, bsz=bsz, n_tiles=n_tiles, emit_state=emit)
    y = _out_proj(x, dn[0], dn[1], z, rg[0], rg[1], gate, mods, dnw, postw, wout,
                  is_lat=is_lat, mod_row0=mod_row0)
    if emit:
        return y, dn[2], rg[2].reshape(bsz, 2, D_RG)
    return y


def kernel(x_prompt, x_sample, state_delta, state_rglru, c, c_ctx, ada_w, ada_b, pre_norm_w, post_norm_w, w_in, w_out, dn_conv_w, dn_a_log, dn_dt_bias, dn_norm_w, rg_conv_w, rg_conv_b, rg_w_r, rg_b_r, rg_w_i, rg_b_i, rg_lam):
    l = 0
    n_lat = c.shape[0]
    c_all = jnp.concatenate([c_ctx[None], c, jnp.zeros((8 - 1 - n_lat, D_MODEL), F32)], axis=0)
    mods = _mods(c_all, ada_w[l], ada_b[l][None])

    w = w_in[l].astype(BF16)
    conv_cols = jnp.concatenate([w[:, :OFF_Z], w[:, OFF_RX:OFF_RX + D_RG]], axis=1).reshape(D_MODEL, N_MIX_SEGS, SEG)
    plain_cols = jnp.concatenate([w[:, OFF_Z:OFF_B], w[:, OFF_RX + D_RG:]], axis=1).reshape(D_MODEL, N_MIX_SEGS, PLAIN)
    win_p = jnp.concatenate([jnp.concatenate([conv_cols, plain_cols], axis=2).reshape(D_MODEL, P_SMALL),
                             _permute_small(w[:, OFF_B:OFF_RX])], axis=1)
    zeros16 = jnp.zeros((2 * N_HEADS,), F32)
    alog_row = _permute_small(jnp.concatenate([zeros16, dn_a_log[l].reshape(-1)]))[None]
    dtb_row = _permute_small(jnp.concatenate([zeros16, dn_dt_bias[l].reshape(-1)]))[None]
    wcat = jnp.concatenate([rg_w_r[l], rg_w_i[l]], axis=-1).astype(BF16)
    weights = (pre_norm_w[l][None], post_norm_w[l][None], win_p, w_out[l].astype(BF16), dn_conv_w[l],
               alog_row, dtb_row, dn_norm_w[l][None], rg_conv_w[l], rg_conv_b[l][None], wcat,
               rg_b_r[l], rg_b_i[l], rg_lam[l])

    y_prompt, s_dn, s_rg = _mixer_path(x_prompt, mods, weights, None, None, is_lat=False, mod_row0=0)
    y_sample = _mixer_path(x_sample, mods, weights, state_delta[:, l], state_rglru[:, l], is_lat=True, mod_row0=1)
    return (y_prompt, y_sample, s_dn[:, None].astype(x_prompt.dtype), s_rg[:, None].astype(x_prompt.dtype))
```
